```python
import jax, jax.numpy as jnp
from jax import lax
import numpy as np

D_MODEL = 1024
BATCH = 2
SEQ = 8192
DEPTH = 4
DEC_BATCH = 128
DEC_SEQ = 4
PAST_LEN = 2048
PAGE_SIZE = 128

N_AB_LAYERS = (DEPTH + 1) // 2
N_C_LAYERS = DEPTH // 2
MIX_WIDTH = D_MODEL
A_GROUPS = 4
A_GROUP_DIM = MIX_WIDTH // 2 // A_GROUPS
A_WIDTH = A_GROUPS * A_GROUP_DIM
GMLP_CHUNK = 128
B_HEADS = 4
B_DK = 128
B_DV = (MIX_WIDTH - A_WIDTH) // B_HEADS
B_QKV = B_HEADS * (2 * B_DK + B_DV)
B_WIDTH = B_HEADS * B_DV
CONV_K = 4
DELTA_CHUNK = 64
AB_IN_WIDTH = 2 * A_WIDTH + B_QKV + 2 * B_HEADS + B_WIDTH
C_HEADS = 16
C_HEAD_DIM = MIX_WIDTH // C_HEADS
QBLOCK = 128
D_FF = 4 * D_MODEL
NORM_EPS = 1e-6

kernel_name = 'hybrid_gmlp_gdn_stickbreak_decode_step'


def rms_norm(x, gain):
    xf = x.astype(jnp.float32)
    y = xf * lax.rsqrt(jnp.mean(xf * xf, axis=-1, keepdims=True) + NORM_EPS)
    return y.astype(x.dtype) * gain


def l2_normalize(x):
    xf = x.astype(jnp.float32)
    return (xf * lax.rsqrt(jnp.sum(xf * xf, axis=-1, keepdims=True) + NORM_EPS)).astype(x.dtype)


def causal_conv_silu(x, buf, w):
    L = x.shape[1]
    xp = jnp.concatenate([buf, x], axis=1)
    out = sum(xp[:, i:i + L] * w[i] for i in range(w.shape[0]))
    return jax.nn.silu(out), xp[:, L:]


def gmlp_spatial_gate(u, v, w_s, b_s):
    Bn, L, G, P = v.shape
    C = min(GMLP_CHUNK, L)
    n = L // C
    w = jnp.where(jnp.tril(jnp.ones((C, C), dtype=bool)), w_s[:, :C, :C], 0.0)
    vc = v.reshape(Bn, n, C, G, P)
    mixed = jnp.einsum('gij,bnjgp->bnigp', w, vc) + b_s[:, :C].T[None, None, :, :, None]
    return u * mixed.reshape(Bn, L, G, P)


def gated_delta_rule(q, k, v, beta, g, S0):
    Bn, L, H, DK = q.shape
    DV = v.shape[-1]
    out_dtype, st_dtype = v.dtype, S0.dtype
    C = DELTA_CHUNK
    pad = (-L) % C
    f32 = jnp.float32

    def prep(t):
        t = t.astype(f32)
        return jnp.pad(t, [(0, 0), (0, pad)] + [(0, 0)] * (t.ndim - 2))

    q, k, v, beta, g = prep(q), prep(k), prep(v), prep(beta), prep(g)
    n = (L + pad) // C
    to_c4 = lambda t: t.reshape(Bn, n, C, H, t.shape[-1]).transpose(1, 0, 3, 2, 4)
    to_c3 = lambda t: t.reshape(Bn, n, C, H).transpose(1, 0, 3, 2)
    qc = to_c4(q) * DK ** -0.5
    kc, vc = to_c4(k), to_c4(v)
    bc = to_c3(beta)
    gc = jnp.cumsum(to_c3(g), axis=-1)
    incl = jnp.tril(jnp.ones((C, C), dtype=bool))
    strict = jnp.tril(jnp.ones((C, C), dtype=bool), k=-1)
    decay = jnp.exp(jnp.where(incl, gc[..., :, None] - gc[..., None, :], -jnp.inf))
    kb = kc * bc[..., None]
    lmat = jnp.where(strict, jnp.einsum('nbhid,nbhjd->nbhij', kb, kc) * decay, 0.0)
    eye = jnp.eye(C, dtype=f32)
    tmat = lax.linalg.triangular_solve(eye + lmat, jnp.broadcast_to(eye, lmat.shape),
                                       left_side=True, lower=True, unit_diagonal=True)
    u = jnp.einsum('nbhij,nbhje->nbhie', tmat, vc * bc[..., None])
    w = jnp.einsum('nbhij,nbhjd->nbhid', tmat, kb * jnp.exp(gc)[..., None])
    a_intra = jnp.einsum('nbhid,nbhjd->nbhij', qc, kc) * decay
    q_dec = qc * jnp.exp(gc)[..., None]
    k_dec = kc * jnp.exp(gc[..., -1:] - gc)[..., None]
    g_last = jnp.exp(gc[..., -1])

    def step(S, xs):
        q_i, k_i, u_i, w_i, a_i, gl_i = xs
        v_new = u_i - jnp.einsum('bhcd,bhde->bhce', w_i, S)
        o_i = jnp.einsum('bhcd,bhde->bhce', q_i, S) + jnp.einsum('bhij,bhje->bhie', a_i, v_new)
        S = S * gl_i[..., None, None] + jnp.einsum('bhcd,bhce->bhde', k_i, v_new)
        return S, o_i

    S_fin, o = lax.scan(step, S0.astype(f32), (q_dec, k_dec, u, w, a_intra, g_last))
    o = o.transpose(1, 0, 3, 2, 4).reshape(Bn, n * C, H, DV)[:, :L]
    return o.astype(out_dtype), S_fin.astype(st_dtype)


def ab_mixer(h, conv_buf, S0, w_in, w_out, w_s, b_s, v_norm, conv_w, a_log, dt_bias, out_norm):
    Bn, L, _ = h.shape
    f32 = jnp.float32
    proj = h @ w_in
    splits = np.cumsum([A_WIDTH, A_WIDTH, B_QKV, B_HEADS, B_HEADS]).tolist()
    a_u, a_v, qkv, beta_raw, a_raw, gate = jnp.split(proj, splits, axis=-1)
    grp = (Bn, L, A_GROUPS, A_GROUP_DIM)
    u = jax.nn.gelu(a_u).reshape(grp)
    v = rms_norm(jax.nn.gelu(a_v).reshape(grp), v_norm.reshape(A_GROUPS, A_GROUP_DIM))
    out_a = gmlp_spatial_gate(u, v, w_s, b_s).reshape(Bn, L, A_WIDTH)
    qkv, new_buf = causal_conv_silu(qkv, conv_buf, conv_w)
    q, k, vb = jnp.split(qkv, [B_HEADS * B_DK, 2 * B_HEADS * B_DK], axis=-1)
    q = l2_normalize(q.reshape(Bn, L, B_HEADS, B_DK))
    k = l2_normalize(k.reshape(Bn, L, B_HEADS, B_DK))
    vb = vb.reshape(Bn, L, B_HEADS, B_DV)
    beta = jax.nn.sigmoid(beta_raw.astype(f32))
    g = -jnp.exp(a_log.astype(f32)) * jax.nn.softplus(a_raw.astype(f32) + dt_bias.astype(f32))
    o, S_new = gated_delta_rule(q, k, vb, beta, g, S0)
    o = rms_norm(o, out_norm) * jax.nn.silu(gate.reshape(Bn, L, B_HEADS, B_DV))
    out = jnp.concatenate([out_a, o.reshape(Bn, L, B_WIDTH)], axis=-1) @ w_out
    return out, new_buf, S_new, v.reshape(Bn, L, A_WIDTH)


def sb_project(h, w_qkv, q_norm, k_norm):
    Bn, L, _ = h.shape
    q, k, v = jnp.split(h @ w_qkv, 3, axis=-1)
    shp = (Bn, L, C_HEADS, C_HEAD_DIM)
    return rms_norm(q.reshape(shp), q_norm), rms_norm(k.reshape(shp), k_norm), v.reshape(shp)


def stick_breaking_weights(z, mask):
    log_keep = jnp.where(mask, -jax.nn.softplus(z), 0.0)
    later = lax.cumsum(log_keep, axis=z.ndim - 1, reverse=True) - log_keep
    return jnp.where(mask, jnp.exp(jax.nn.log_sigmoid(z) + later), 0.0)


def sb_attend_prompt(q, k, v, logit_bias):
    Bn, L, H, Dh = q.shape
    qb = min(QBLOCK, L)
    nb = L // qb
    kf, vf = k.astype(jnp.float32), v.astype(jnp.float32)
    bias = logit_bias.astype(jnp.float32)[None, :, None, None]
    k_pos = jnp.arange(L)

    def block(args):
        q_blk, start = args
        z = jnp.einsum('bqhd,bkhd->bhqk', q_blk.astype(jnp.float32), kf) * Dh ** -0.5 + bias
        mask = k_pos[None, :] < (start + jnp.arange(qb))[:, None]
        return jnp.einsum('bhqk,bkhd->bqhd', stick_breaking_weights(z, mask), vf)

    q_blocks = q.reshape(Bn, nb, qb, H, Dh).transpose(1, 0, 2, 3, 4)
    o = lax.map(block, (q_blocks, jnp.arange(nb) * qb))
    return o.transpose(1, 0, 2, 3, 4).reshape(Bn, L, H, Dh).astype(v.dtype)


def sb_attend_sample(q, k_new, v_new, k_past, v_past, logit_bias):
    Bn, T, H, Dh = q.shape
    P = k_past.shape[1]
    f32 = jnp.float32
    bias = logit_bias.astype(f32)[None, :, None, None]
    z = jnp.concatenate([jnp.einsum('bqhd,bkhd->bhqk', q, k_past, preferred_element_type=f32),
                         jnp.einsum('bqhd,bkhd->bhqk', q, k_new, preferred_element_type=f32)], axis=-1) * Dh ** -0.5 + bias
    mask = jnp.arange(P + T)[None, :] < (P + jnp.arange(T))[:, None]
    w = stick_breaking_weights(z, mask)
    o = (jnp.einsum('bhqk,bkhd->bqhd', w[..., :P], v_past.astype(f32))
         + jnp.einsum('bhqk,bkhd->bqhd', w[..., P:], v_new.astype(f32)))
    return o.astype(v_new.dtype)


def sq_relu_mlp(h, w_up, w_down):
    return jnp.square(jax.nn.relu(h @ w_up)) @ w_down


def setup_inputs(seed: int = 0) -> dict:
    key = jax.random.key(seed)
    ks = jax.random.split(key, 25)
    n_pages = PAST_LEN // PAGE_SIZE
    n_pool = (5 * DEC_BATCH * n_pages) // 4

    def nrm(k, shape, scale=1.0):
        return scale * jax.random.normal(k, shape, jnp.float32)

    x_prompt = nrm(ks[0], (BATCH, SEQ, D_MODEL))
    x_sample = nrm(ks[1], (DEC_BATCH, DEC_SEQ, D_MODEL))
    cache_k = nrm(ks[2], (N_C_LAYERS, n_pool, PAGE_SIZE, C_HEADS, C_HEAD_DIM))
    cache_v = nrm(ks[3], (N_C_LAYERS, n_pool, PAGE_SIZE, C_HEADS, C_HEAD_DIM))
    state_delta = nrm(ks[4], (N_AB_LAYERS, DEC_BATCH, B_HEADS, B_DK, B_DV), 0.1)
    state_conv = nrm(ks[5], (N_AB_LAYERS, DEC_BATCH, CONV_K - 1, B_QKV))
    page_table = jax.random.permutation(ks[6], n_pool)[: DEC_BATCH * n_pages].reshape(DEC_BATCH, n_pages).astype(jnp.int32)
    norm_mix = 1.0 + nrm(ks[7], (DEPTH, D_MODEL), 0.05)
    norm_mlp = 1.0 + nrm(ks[8], (DEPTH, D_MODEL), 0.05)
    ab_w_in = nrm(ks[9], (N_AB_LAYERS, D_MODEL, AB_IN_WIDTH), D_MODEL ** -0.5)
    ab_w_out = nrm(ks[10], (N_AB_LAYERS, MIX_WIDTH, D_MODEL), MIX_WIDTH ** -0.5)
    gmlp_w_s = nrm(ks[11], (N_AB_LAYERS, A_GROUPS, GMLP_CHUNK, GMLP_CHUNK), GMLP_CHUNK ** -0.5)
    gmlp_b_s = 1.0 + nrm(ks[12], (N_AB_LAYERS, A_GROUPS, GMLP_CHUNK), 0.05)
    gmlp_v_norm = 1.0 + nrm(ks[13], (N_AB_LAYERS, A_WIDTH), 0.05)
    gdn_conv_w = nrm(ks[14], (N_AB_LAYERS, CONV_K, B_QKV), CONV_K ** -0.5)
    gdn_a_log = jnp.log(jax.random.uniform(ks[15], (N_AB_LAYERS, B_HEADS), jnp.float32, minval=1.0, maxval=16.0))
    gdn_dt_bias = jax.random.uniform(ks[16], (N_AB_LAYERS, B_HEADS), jnp.float32, minval=-4.0, maxval=-1.0)
    gdn_out_norm = 1.0 + nrm(ks[17], (N_AB_LAYERS, B_DV), 0.05)
    sb_w_qkv = nrm(ks[18], (N_C_LAYERS, D_MODEL, 3 * MIX_WIDTH), D_MODEL ** -0.5)
    sb_w_out = nrm(ks[19], (N_C_LAYERS, MIX_WIDTH, D_MODEL), MIX_WIDTH ** -0.5)
    sb_q_norm = 1.0 + nrm(ks[20], (N_C_LAYERS, C_HEAD_DIM), 0.05)
    sb_k_norm = 1.0 + nrm(ks[21], (N_C_LAYERS, C_HEAD_DIM), 0.05)
    sb_logit_bias = jax.random.uniform(ks[24], (N_C_LAYERS, C_HEADS), jnp.float32, minval=-8.0, maxval=-6.0)
    mlp_w_up = nrm(ks[22], (DEPTH, D_MODEL, D_FF), D_MODEL ** -0.5)
    mlp_w_down = nrm(ks[23], (DEPTH, D_FF, D_MODEL), D_FF ** -0.5)
    return {'x_prompt': x_prompt, 'x_sample': x_sample, 'cache_k': cache_k, 'cache_v': cache_v,
            'state_delta': state_delta, 'state_conv': state_conv, 'page_table': page_table,
            'norm_mix': norm_mix, 'norm_mlp': norm_mlp, 'ab_w_in': ab_w_in, 'ab_w_out': ab_w_out,
            'gmlp_w_s': gmlp_w_s, 'gmlp_b_s': gmlp_b_s, 'gmlp_v_norm': gmlp_v_norm,
            'gdn_conv_w': gdn_conv_w, 'gdn_a_log': gdn_a_log, 'gdn_dt_bias': gdn_dt_bias, 'gdn_out_norm': gdn_out_norm,
            'sb_w_qkv': sb_w_qkv, 'sb_w_out': sb_w_out, 'sb_q_norm': sb_q_norm, 'sb_k_norm': sb_k_norm,
            'sb_logit_bias': sb_logit_bias, 'mlp_w_up': mlp_w_up, 'mlp_w_down': mlp_w_down}


def reference(x_prompt, x_sample, cache_k, cache_v, state_delta, state_conv, page_table,
              norm_mix, norm_mlp, ab_w_in, ab_w_out, gmlp_w_s, gmlp_b_s, gmlp_v_norm,
              gdn_conv_w, gdn_a_log, gdn_dt_bias, gdn_out_norm,
              sb_w_qkv, sb_w_out, sb_q_norm, sb_k_norm, sb_logit_bias, mlp_w_up, mlp_w_down):
    bp, bs = x_prompt.shape[0], x_sample.shape[0]
    past = page_table.shape[1] * PAGE_SIZE
    y_p, y_s = x_prompt, x_sample
    k_pr, v_pr, k_sa, v_sa = [], [], [], []
    d_pr, d_sa, c_pr, c_sa, gv_sa = [], [], [], [], []
    for layer in range(DEPTH):
        li = layer // 2
        h_p = rms_norm(y_p, norm_mix[layer])
        h_s = rms_norm(y_s, norm_mix[layer])
        if layer % 2 == 0:
            w = (ab_w_in[li], ab_w_out[li], gmlp_w_s[li], gmlp_b_s[li], gmlp_v_norm[li],
                 gdn_conv_w[li], gdn_a_log[li], gdn_dt_bias[li], gdn_out_norm[li])
            conv0 = jnp.zeros((bp, CONV_K - 1, B_QKV), x_prompt.dtype)
            s0 = jnp.zeros((bp, B_HEADS, B_DK, B_DV), x_prompt.dtype)
            o_p, cp, dp, _ = ab_mixer(h_p, conv0, s0, *w)
            o_s, cs, ds, vs = ab_mixer(h_s, state_conv[li], state_delta[li], *w)
            c_pr.append(cp); d_pr.append(dp)
            c_sa.append(cs); d_sa.append(ds); gv_sa.append(vs)
        else:
            q, k, v = sb_project(h_p, sb_w_qkv[li], sb_q_norm[li], sb_k_norm[li])
            o_p = sb_attend_prompt(q, k, v, sb_logit_bias[li]).reshape(bp, -1, MIX_WIDTH) @ sb_w_out[li]
            k_pr.append(k); v_pr.append(v)
            q, k, v = sb_project(h_s, sb_w_qkv[li], sb_q_norm[li], sb_k_norm[li])
            k_past = cache_k[li][page_table].reshape(bs, past, C_HEADS, C_HEAD_DIM)
            v_past = cache_v[li][page_table].reshape(bs, past, C_HEADS, C_HEAD_DIM)
            o_s = sb_attend_sample(q, k, v, k_past, v_past, sb_logit_bias[li]).reshape(bs, -1, MIX_WIDTH) @ sb_w_out[li]
            k_sa.append(k); v_sa.append(v)
        y_p = y_p + o_p
        y_s = y_s + o_s
        y_p = y_p + sq_relu_mlp(rms_norm(y_p, norm_mlp[layer]), mlp_w_up[layer], mlp_w_down[layer])
        y_s = y_s + sq_relu_mlp(rms_norm(y_s, norm_mlp[layer]), mlp_w_up[layer], mlp_w_down[layer])
    return (y_p, y_s,
            jnp.stack(k_pr), jnp.stack(v_pr), jnp.stack(k_sa), jnp.stack(v_sa),
            jnp.stack(d_pr), jnp.stack(d_sa), jnp.stack(c_pr), jnp.stack(c_sa), jnp.stack(gv_sa))
```

```python
import functools

import jax
import jax.numpy as jnp
from jax import lax
from jax.experimental import pallas as pl
from jax.experimental.pallas import tpu as pltpu

F32 = jnp.float32
BF16 = jnp.bfloat16
HI = lax.Precision.HIGHEST
NORM_EPS = 1e-6
LANES = 128
SUBLANES = 8
VMEM_LIMIT_BYTES = 48 * 1024 * 1024

GMLP_CHUNK = 128
GDN_CHUNK = 64
CONV_K = 4
ROW_TILE = 512
ATT_BLOCK = 128
PAD_TOK = 8


def _params(semantics):
    return pltpu.CompilerParams(dimension_semantics=semantics, vmem_limit_bytes=VMEM_LIMIT_BYTES)


def _dot(a, b):
    return jnp.dot(a.astype(BF16), b.astype(BF16), preferred_element_type=F32)


def _dot_nt(a, b):
    return lax.dot_general(a.astype(BF16), b.astype(BF16), (((1,), (1,)), ((), ())),
                           preferred_element_type=F32)


def _dot_tn(a, b):
    return lax.dot_general(a.astype(BF16), b.astype(BF16), (((0,), (0,)), ((), ())),
                           preferred_element_type=F32)


def _dot32(a, b):
    return jnp.dot(a, b, precision=HI, preferred_element_type=F32)


def _dot32_nt(a, b):
    return lax.dot_general(a, b, (((1,), (1,)), ((), ())), precision=HI, preferred_element_type=F32)


def _split_dot(x, ones_mat):
    hi = x.astype(BF16)
    lo = (x - hi.astype(F32)).astype(BF16)
    return (jnp.dot(hi, ones_mat, preferred_element_type=F32)
            + jnp.dot(lo, ones_mat, preferred_element_type=F32))


def _sigmoid(x):
    return 1.0 / (1.0 + jnp.exp(-x))


def _silu(x):
    return x * _sigmoid(x)


def _softplus(x):
    return jnp.maximum(x, 0.0) + jnp.log1p(jnp.exp(-jnp.abs(x)))


def _gelu_tanh(x):
    return x * (0.5 * (1.0 + jnp.tanh(0.7978845608028654 * (x + 0.044715 * (x * x * x)))))


def _rms_rows(x, gain):
    ms = jnp.mean(x * x, axis=-1, keepdims=True)
    return x * lax.rsqrt(ms + NORM_EPS) * gain


def _norm_matmul_kernel(x_ref, g_ref, w_ref, o_ref, hn_ref):
    @pl.when(pl.program_id(1) == 0)
    def _():
        hn_ref[...] = _rms_rows(x_ref[...], g_ref[...]).astype(BF16)

    o_ref[...] = jnp.dot(hn_ref[...], w_ref[...], preferred_element_type=F32)


def _norm_matmul(x, gain, w, tm, tn):
    t, d = x.shape
    n = w.shape[1]
    return pl.pallas_call(
        _norm_matmul_kernel,
        grid=(t // tm, n // tn),
        in_specs=[pl.BlockSpec((tm, d), lambda i, j: (i, 0)),
                  pl.BlockSpec((1, d), lambda i, j: (0, 0)),
                  pl.BlockSpec((d, tn), lambda i, j: (0, j))],
        out_specs=pl.BlockSpec((tm, tn), lambda i, j: (i, j)),
        out_shape=jax.ShapeDtypeStruct((t, n), F32),
        scratch_shapes=[pltpu.VMEM((tm, d), BF16)],
        compiler_params=_params(("parallel", "arbitrary")),
        name="norm_matmul",
    )(x, gain, w)


def _matmul_residual_kernel(n_lhs, *refs):
    res_ref = refs[0]
    a_refs = refs[1:1 + n_lhs]
    w_refs = refs[1 + n_lhs:1 + 2 * n_lhs]
    o_ref = refs[1 + 2 * n_lhs]
    acc = res_ref[...]
    for a_ref, w_ref in zip(a_refs, w_refs):
        acc = acc + jnp.dot(a_ref[...].astype(BF16), w_ref[...], preferred_element_type=F32)
    o_ref[...] = acc


def _matmul_residual(res, lhs_list, w_list, tm):
    t, d = res.shape
    n_lhs = len(lhs_list)
    in_specs = [pl.BlockSpec((tm, d), lambda i: (i, 0))]
    in_specs += [pl.BlockSpec((tm, a.shape[1]), lambda i: (i, 0)) for a in lhs_list]
    in_specs += [pl.BlockSpec(w.shape, lambda i: (0, 0)) for w in w_list]
    return pl.pallas_call(
        functools.partial(_matmul_residual_kernel, n_lhs),
        grid=(t // tm,),
        in_specs=in_specs,
        out_specs=pl.BlockSpec((tm, d), lambda i: (i, 0)),
        out_shape=jax.ShapeDtypeStruct((t, d), F32),
        compiler_params=_params(("parallel",)),
        name="matmul_residual",
    )(res, *lhs_list, *w_list)


def _mlp_kernel(x_ref, g_ref, wu_ref, wd_ref, o_ref, hn_ref):
    f = pl.program_id(1)

    @pl.when(f == 0)
    def _():
        x = x_ref[...]
        hn_ref[...] = _rms_rows(x, g_ref[...]).astype(BF16)
        o_ref[...] = x

    h = jnp.dot(hn_ref[...], wu_ref[...], preferred_element_type=F32)
    h = jnp.maximum(h, 0.0)
    h = (h * h).astype(BF16)
    o_ref[...] += jnp.dot(h, wd_ref[...], preferred_element_type=F32)


def _mlp(x, gain, w_up, w_down, tm, tf):
    t, d = x.shape
    dff = w_up.shape[1]
    return pl.pallas_call(
        _mlp_kernel,
        grid=(t // tm, dff // tf),
        in_specs=[pl.BlockSpec((tm, d), lambda i, f: (i, 0)),
                  pl.BlockSpec((1, d), lambda i, f: (0, 0)),
                  pl.BlockSpec((d, tf), lambda i, f: (0, f)),
                  pl.BlockSpec((tf, d), lambda i, f: (f, 0))],
        out_specs=pl.BlockSpec((tm, d), lambda i, f: (i, 0)),
        out_shape=jax.ShapeDtypeStruct((t, d), F32),
        scratch_shapes=[pltpu.VMEM((tm, d), BF16)],
        compiler_params=_params(("parallel", "arbitrary")),
        name="mlp",
    )(x, gain, w_up, w_down)


def _gmlp_kernel(n_chunks, n_groups, u_ref, v_ref, vn_ref, wmix_ref, bias_ref, oa_ref, vout_ref):
    for c in range(n_chunks):
        rows = slice(c * GMLP_CHUNK, (c + 1) * GMLP_CHUNK)
        for g in range(n_groups):
            cols = slice(g * LANES, (g + 1) * LANES)
            vn = _rms_rows(_gelu_tanh(v_ref[rows, cols]), vn_ref[:, cols])
            vout_ref[rows, cols] = vn
            mixed = jnp.dot(wmix_ref[g], vn.astype(BF16), preferred_element_type=F32) + bias_ref[g]
            oa_ref[rows, cols] = _gelu_tanh(u_ref[rows, cols]) * mixed


def _gmlp(proj, u_blk, v_blk, v_norm, wmix, bias_full, rows_per_step):
    t = proj.shape[0]
    n_groups = wmix.shape[0]
    width = n_groups * LANES
    n_chunks = rows_per_step // GMLP_CHUNK
    return pl.pallas_call(
        functools.partial(_gmlp_kernel, n_chunks, n_groups),
        grid=(t // rows_per_step,),
        in_specs=[pl.BlockSpec((rows_per_step, width), lambda i: (i, u_blk)),
                  pl.BlockSpec((rows_per_step, width), lambda i: (i, v_blk)),
                  pl.BlockSpec((1, width), lambda i: (0, 0)),
                  pl.BlockSpec(wmix.shape, lambda i: (0, 0, 0)),
                  pl.BlockSpec(bias_full.shape, lambda i: (0, 0, 0))],
        out_specs=[pl.BlockSpec((rows_per_step, width), lambda i: (i, 0)),
                   pl.BlockSpec((rows_per_step, width), lambda i: (i, 0))],
        out_shape=[jax.ShapeDtypeStruct((t, width), F32), jax.ShapeDtypeStruct((t, width), F32)],
        compiler_params=_params(("parallel",)),
        name="gmlp",
    )(proj, proj, v_norm, wmix, bias_full)


def _l2norm_rows(x):
    return x * lax.rsqrt(jnp.sum(x * x, axis=-1, keepdims=True) + NORM_EPS)


def _gdn_gates(sm, alog, dtb):
    beta = _sigmoid(sm[:, :LANES])
    g = -jnp.exp(alog) * _softplus(sm[:, LANES:] + dtb)
    return beta, g


def _gdn_out(o, onorm, gate):
    return _rms_rows(o, onorm) * _silu(gate)


def _transpose_rows(x):
    rows = x.shape[0]
    if rows < LANES:
        x = jnp.concatenate([x, jnp.zeros((LANES - rows, LANES), F32)], axis=0)
    return jnp.transpose(x)


def _inv_unit_lower(lmat, size):
    r = lax.broadcasted_iota(jnp.int32, (size, size), 0)
    c = lax.broadcasted_iota(jnp.int32, (size, size), 1)
    x = -lmat
    p = jnp.where(r == c, 1.0, 0.0) + x
    n = 2
    while n < size:
        x = _dot32(x, x)
        p = p + _dot32(p, x)
        n *= 2
    return p


def _gdn_prompt_kernel(n_heads, chunk, x_ref, gate_ref, sm_ref, cw_ref, alog_ref, dtb_ref, onorm_ref,
                       o_ref, s_out_ref, xbuf_ref, s_ref):
    c_idx = pl.program_id(1)
    hw = n_heads * LANES

    @pl.when(c_idx == 0)
    def _():
        xbuf_ref[0:SUBLANES, :] = jnp.zeros((SUBLANES, xbuf_ref.shape[1]), F32)
        s_ref[...] = jnp.zeros(s_ref.shape, F32)

    x = x_ref[...]
    xbuf_ref[SUBLANES:SUBLANES + chunk, :] = x
    conv = cw_ref[CONV_K - 1:CONV_K, :] * x
    for i in range(CONV_K - 1):
        shift = CONV_K - 1 - i
        conv = conv + cw_ref[i:i + 1, :] * xbuf_ref[SUBLANES - shift:SUBLANES - shift + chunk, :]
    xbuf_ref[0:SUBLANES, :] = x[chunk - SUBLANES:chunk, :]
    act = _silu(conv)

    beta, g = _gdn_gates(sm_ref[...], alog_ref[...], dtb_ref[...])
    r = lax.broadcasted_iota(jnp.int32, (chunk, chunk), 0)
    c = lax.broadcasted_iota(jnp.int32, (chunk, chunk), 1)
    incl = c <= r
    strict = c < r
    gc = _dot32(jnp.where(incl, 1.0, 0.0), g)
    gc_t = _transpose_rows(gc)

    for h in range(n_heads):
        cols = slice(h * LANES, (h + 1) * LANES)
        q_h = _l2norm_rows(act[:, h * LANES:(h + 1) * LANES]) * (LANES ** -0.5)
        k_h = _l2norm_rows(act[:, hw + h * LANES:hw + (h + 1) * LANES])
        v_h = act[:, 2 * hw + h * LANES:2 * hw + (h + 1) * LANES]
        b_col = beta[:, h:h + 1]
        gc_col = gc[:, h:h + 1]
        gc_row = gc_t[h:h + 1, :chunk]
        decay = jnp.where(incl, jnp.exp(jnp.where(incl, gc_col - gc_row, 0.0)), 0.0)
        egc = jnp.exp(gc_col)
        g_last = gc[chunk - 1:chunk, h:h + 1]
        kb = k_h * b_col
        lmat = jnp.where(strict, _dot32_nt(kb, k_h) * decay, 0.0)
        tmat = _inv_unit_lower(lmat, chunk)
        u = _dot32(tmat, v_h * b_col)
        w = _dot32(tmat, kb * egc)
        a_in = _dot32_nt(q_h, k_h) * decay
        q_dec = q_h * egc
        k_dec = k_h * jnp.exp(g_last - gc_col)
        s = s_ref[h]
        v_new = u - _dot32(w, s)
        o = _dot32(q_dec, s) + _dot32(a_in, v_new)
        v_pad = jnp.concatenate([v_new, jnp.zeros((LANES - chunk, LANES), F32)], axis=0)
        s_ref[h] = s * jnp.exp(g_last) + _dot32(_transpose_rows(k_dec), v_pad)
        o_ref[:, cols] = _gdn_out(o, onorm_ref[...], gate_ref[:, cols])

    @pl.when(c_idx == pl.num_programs(1) - 1)
    def _():
        s_out_ref[0] = s_ref[...]


def _gdn_prompt(proj, n_batch, seq, n_heads, gate_blk, sm_blk, conv_w, alog, dtb, onorm):
    chunk = GDN_CHUNK
    nc = seq // chunk
    qkv_w = 3 * n_heads * LANES
    hw = n_heads * LANES
    return pl.pallas_call(
        functools.partial(_gdn_prompt_kernel, n_heads, chunk),
        grid=(n_batch, nc),
        in_specs=[pl.BlockSpec((chunk, qkv_w), lambda b, c: (b * nc + c, 0)),
                  pl.BlockSpec((chunk, hw), lambda b, c: (b * nc + c, gate_blk)),
                  pl.BlockSpec((chunk, 2 * LANES), lambda b, c: (b * nc + c, sm_blk)),
                  pl.BlockSpec((CONV_K, qkv_w), lambda b, c: (0, 0)),
                  pl.BlockSpec((1, LANES), lambda b, c: (0, 0)),
                  pl.BlockSpec((1, LANES), lambda b, c: (0, 0)),
                  pl.BlockSpec((1, LANES), lambda b, c: (0, 0))],
        out_specs=[pl.BlockSpec((chunk, hw), lambda b, c: (b * nc + c, 0)),
                   pl.BlockSpec((1, n_heads, LANES, LANES), lambda b, c: (b, 0, 0, 0))],
        out_shape=[jax.ShapeDtypeStruct((n_batch * seq, hw), F32),
                   jax.ShapeDtypeStruct((n_batch, n_heads, LANES, LANES), F32)],
        scratch_shapes=[pltpu.VMEM((SUBLANES + chunk, qkv_w), F32),
                        pltpu.VMEM((n_heads, LANES, LANES), F32)],
        compiler_params=_params(("parallel", "arbitrary")),
        name="gdn_prompt",
    )(proj, proj, proj, conv_w, alog, dtb, onorm)


def _gdn_sample_kernel(n_heads, n_tok, bb, xp_ref, gate_ref, sm_ref, cw_ref, alog_ref, dtb_ref, onorm_ref,
                       s0_ref, o_ref, s_out_ref):
    hw = n_heads * LANES

    def one_request(b, carry):
        conv = cw_ref[0:1, :] * xp_ref[b, 0:PAD_TOK, :]
        for i in range(1, CONV_K):
            conv = conv + cw_ref[i:i + 1, :] * xp_ref[b, i:i + PAD_TOK, :]
        act = _silu(conv)
        beta, g = _gdn_gates(sm_ref[b], alog_ref[...], dtb_ref[...])
        eg = jnp.exp(g)
        gate = gate_ref[b]
        ks = [_l2norm_rows(act[:, hw + h * LANES:hw + (h + 1) * LANES]) for h in range(n_heads)]
        qs = [_l2norm_rows(act[:, h * LANES:(h + 1) * LANES]) * (LANES ** -0.5) for h in range(n_heads)]
        kq_t = _transpose_rows(jnp.concatenate(ks + qs, axis=0))
        for h in range(n_heads):
            v_h = act[:, 2 * hw + h * LANES:2 * hw + (h + 1) * LANES]
            s = s0_ref[b, h]
            outs = []
            for t in range(n_tok):
                kc = kq_t[:, h * PAD_TOK + t:h * PAD_TOK + t + 1]
                qc = kq_t[:, (n_heads + h) * PAD_TOK + t:(n_heads + h) * PAD_TOK + t + 1]
                e = eg[t:t + 1, h:h + 1]
                ks = jnp.sum(s * kc, axis=0, keepdims=True)
                d = beta[t:t + 1, h:h + 1] * (v_h[t:t + 1, :] - e * ks)
                s = e * s + kc * d
                outs.append(jnp.sum(s * qc, axis=0, keepdims=True))
            outs.append(jnp.zeros((PAD_TOK - n_tok, LANES), F32))
            o = jnp.concatenate(outs, axis=0)
            s_out_ref[b, h] = s
            o_ref[b, :, h * LANES:(h + 1) * LANES] = _gdn_out(
                o, onorm_ref[...], gate[:, h * LANES:(h + 1) * LANES])
        return carry

    lax.fori_loop(0, bb, one_request, 0)


def _gdn_sample(xp, gate, sm, conv_w, alog, dtb, onorm, s0, n_tok, bb):
    nb, n_heads = s0.shape[0], s0.shape[1]
    hw = n_heads * LANES
    qkv_w = 3 * hw
    return pl.pallas_call(
        functools.partial(_gdn_sample_kernel, n_heads, n_tok, bb),
        grid=(nb // bb,),
        in_specs=[pl.BlockSpec((bb, 2 * PAD_TOK, qkv_w), lambda i: (i, 0, 0)),
                  pl.BlockSpec((bb, PAD_TOK, hw), lambda i: (i, 0, 0)),
                  pl.BlockSpec((bb, PAD_TOK, 2 * LANES), lambda i: (i, 0, 0)),
                  pl.BlockSpec((CONV_K, qkv_w), lambda i: (0, 0)),
                  pl.BlockSpec((1, LANES), lambda i: (0, 0)),
                  pl.BlockSpec((1, LANES), lambda i: (0, 0)),
                  pl.BlockSpec((1, LANES), lambda i: (0, 0)),
                  pl.BlockSpec((bb, n_heads, LANES, LANES), lambda i: (i, 0, 0, 0))],
        out_specs=[pl.BlockSpec((bb, PAD_TOK, hw), lambda i: (i, 0, 0)),
                   pl.BlockSpec((bb, n_heads, LANES, LANES), lambda i: (i, 0, 0, 0))],
        out_shape=[jax.ShapeDtypeStruct((nb, PAD_TOK, hw), F32),
                   jax.ShapeDtypeStruct(s0.shape, F32)],
        compiler_params=_params(("parallel",)),
        name="gdn_sample",
    )(xp, gate, sm, conv_w, alog, dtb, onorm, s0)


def _head_rms(r, gain, seg_ones, head_dim):
    ss = _split_dot(r * r, seg_ones)
    return r * lax.rsqrt(ss * (1.0 / head_dim) + NORM_EPS) * gain


def _sb_qkv_kernel(head_dim, x_ref, g_ref, w_ref, qn_ref, kn_ref, seg_ref,
                   qb_ref, k_ref, v_ref, kb_ref, vb_ref, hn_ref):
    j = pl.program_id(1)

    @pl.when(j == 0)
    def _():
        hn_ref[...] = _rms_rows(x_ref[...], g_ref[...]).astype(BF16)

    r = jnp.dot(hn_ref[...], w_ref[...], preferred_element_type=F32)
    n_blk = r.shape[1] // LANES
    seg = seg_ref[...]

    @pl.when(j == 0)
    def _():
        for c in range(n_blk):
            cols = slice(c * LANES, (c + 1) * LANES)
            qn = _head_rms(r[:, cols], qn_ref[...], seg, head_dim)
            qb_ref[:, cols] = (qn * (head_dim ** -0.5)).astype(BF16)

    @pl.when(j == 1)
    def _():
        for c in range(n_blk):
            cols = slice(c * LANES, (c + 1) * LANES)
            kn = _head_rms(r[:, cols], kn_ref[...], seg, head_dim)
            k_ref[:, cols] = kn
            kb_ref[:, cols] = kn.astype(BF16)

    @pl.when(j == 2)
    def _():
        v_ref[...] = r
        vb_ref[...] = r.astype(BF16)


def _sb_qkv(x, gain, w, q_gain, k_gain, seg_ones, head_dim, tm):
    t, d = x.shape
    width = w.shape[1] // 3
    row_blk = pl.BlockSpec((tm, width), lambda i, j: (i, 0))
    return pl.pallas_call(
        functools.partial(_sb_qkv_kernel, head_dim),
        grid=(t // tm, 3),
        in_specs=[pl.BlockSpec((tm, d), lambda i, j: (i, 0)),
                  pl.BlockSpec((1, d), lambda i, j: (0, 0)),
                  pl.BlockSpec((d, width), lambda i, j: (0, j)),
                  pl.BlockSpec((1, LANES), lambda i, j: (0, 0)),
                  pl.BlockSpec((1, LANES), lambda i, j: (0, 0)),
                  pl.BlockSpec((LANES, LANES), lambda i, j: (0, 0))],
        out_specs=[row_blk, row_blk, row_blk, row_blk, row_blk],
        out_shape=[jax.ShapeDtypeStruct((t, width), BF16),
                   jax.ShapeDtypeStruct((t, width), F32),
                   jax.ShapeDtypeStruct((t, width), F32),
                   jax.ShapeDtypeStruct((t, width), BF16),
                   jax.ShapeDtypeStruct((t, width), BF16)],
        scratch_shapes=[pltpu.VMEM((tm, d), BF16)],
        compiler_params=_params(("parallel", "arbitrary")),
        name="sb_qkv",
    )(x, gain, w, q_gain, k_gain, seg_ones)


def _stick_block(z, carry, cum_ones, mask):
    tk = z.shape[1]
    sp = _softplus(z)
    lk = -sp if mask is None else jnp.where(mask, -sp, 0.0)
    cs = _split_dot(lk, cum_ones)
    a = jnp.exp((z - sp) + cs[:, :tk] + carry)
    if mask is not None:
        a = jnp.where(mask, a, 0.0)
    return a, carry + cs[:, tk:]


def _attn_prompt_kernel(head_dim, bias_ref, q_ref, k_ref, v_ref, cum_ref, o_ref):
    p = pl.program_id(1)
    qi = pl.program_id(2)
    tq = q_ref.shape[0]
    tk = tq
    q = q_ref[...].astype(F32)
    cum_ones = cum_ref[...]
    lane = lax.broadcasted_iota(jnp.int32, (tq, LANES), 1)
    row = lax.broadcasted_iota(jnp.int32, (tq, tk), 0)
    col = lax.broadcasted_iota(jnp.int32, (tq, tk), 1)
    causal = col < row
    heads_per_blk = LANES // head_dim
    out = jnp.zeros((tq, LANES), F32)
    for hh in range(heads_per_blk):
        in_head = (lane >= hh * head_dim) & (lane < (hh + 1) * head_dim)
        qm = jnp.where(in_head, q, 0.0).astype(BF16)
        bias = bias_ref[p * heads_per_blk + hh]

        def block(j, carry, acc, mask):
            start = pl.multiple_of(j * tk, tk)
            kblk = k_ref[pl.ds(start, tk), :]
            vblk = v_ref[pl.ds(start, tk), :]
            z = lax.dot_general(qm, kblk, (((1,), (1,)), ((), ())), preferred_element_type=F32) + bias
            a, carry = _stick_block(z, carry, cum_ones, mask)
            acc = acc + jnp.dot(a.astype(BF16), vblk, preferred_element_type=F32)
            return carry, acc

        zeros = jnp.zeros((tq, LANES), F32)
        carry, acc = block(qi, zeros, zeros, causal)

        def body(t, state):
            return block(qi - 1 - t, state[0], state[1], None)

        carry, acc = lax.fori_loop(0, qi, body, (carry, acc))
        out = jnp.where(in_head, acc, out)
    o_ref[...] = out


def _attn_prompt(qb, kb, vb, bias, cum_ones, n_batch, seq, head_dim):
    width = qb.shape[1]
    tq = ATT_BLOCK
    nq = seq // tq
    return pl.pallas_call(
        functools.partial(_attn_prompt_kernel, head_dim),
        grid=(n_batch, width // LANES, nq),
        in_specs=[pl.BlockSpec(memory_space=pltpu.SMEM),
                  pl.BlockSpec((tq, LANES), lambda b, p, i: (b * nq + i, p)),
                  pl.BlockSpec((seq, LANES), lambda b, p, i: (b, p)),
                  pl.BlockSpec((seq, LANES), lambda b, p, i: (b, p)),
                  pl.BlockSpec(cum_ones.shape, lambda b, p, i: (0, 0))],
        out_specs=pl.BlockSpec((tq, LANES), lambda b, p, i: (b * nq + i, p)),
        out_shape=jax.ShapeDtypeStruct((n_batch * seq, width), F32),
        compiler_params=_params(("parallel", "parallel", "arbitrary")),
        name="attn_prompt",
    )(bias, qb, kb, vb, cum_ones)


def _attn_sample_kernel(head_dim, n_tok, pt_ref, q_ref, kn_ref, vn_ref, k_ref, v_ref, bias_ref, cum_ref,
                        o_ref, qbd_ref, acc_ref, carry_ref):
    j = pl.program_id(1)
    n_rows, width = qbd_ref.shape
    n_heads = width // head_dim
    row_head = lax.shift_right_logical(lax.broadcasted_iota(jnp.int32, (n_rows, width), 0),
                                       PAD_TOK.bit_length() - 1)
    col_head = lax.shift_right_logical(lax.broadcasted_iota(jnp.int32, (n_rows, width), 1),
                                       head_dim.bit_length() - 1)
    own = row_head == col_head
    bias = bias_ref[...]

    @pl.when(j == 0)
    def _():
        q8 = q_ref[0]
        qbd = jnp.where(own, jnp.concatenate([q8] * n_heads, axis=0), 0.0)
        qbd_ref[...] = qbd.astype(BF16)
        tok = lax.broadcasted_iota(jnp.int32, (n_rows, 1), 0) & (PAD_TOK - 1)
        bcol = bias[:, 0:1]
        lks, lss, valids = [], [], []
        for s in range(n_tok):
            z = jnp.sum(qbd * kn_ref[0, s:s + 1, :], axis=-1, keepdims=True) + bcol
            sp = _softplus(z)
            valid = tok > s
            valids.append(valid)
            lks.append(jnp.where(valid, -sp, 0.0))
            lss.append(z - sp)
        later = jnp.zeros((n_rows, 1), F32)
        acc = jnp.zeros((n_rows, width), F32)
        for s in reversed(range(n_tok)):
            a = jnp.where(valids[s], jnp.exp(lss[s] + later), 0.0)
            acc = acc + a * vn_ref[0, s:s + 1, :]
            later = later + lks[s]
        acc_ref[...] = acc
        carry_ref[...] = jnp.broadcast_to(later, carry_ref.shape)

    kp = k_ref[...].astype(BF16)
    vp = v_ref[...].astype(BF16)
    z = lax.dot_general(qbd_ref[...], kp, (((1,), (1,)), ((), ())), preferred_element_type=F32) + bias
    a, carry = _stick_block(z, carry_ref[...], cum_ref[...], None)
    carry_ref[...] = carry
    acc_ref[...] += jnp.dot(a.astype(BF16), vp, preferred_element_type=F32)

    @pl.when(j == pl.num_programs(1) - 1)
    def _():
        acc = jnp.where(own, acc_ref[...], 0.0)
        o = acc[0:PAD_TOK, :]
        for h in range(1, n_heads):
            o = o + acc[h * PAD_TOK:(h + 1) * PAD_TOK, :]
        o_ref[0] = o


def _attn_sample(page_table, q_pad, kn_pad, vn_pad, cache_k, cache_v, layer, bias_rows, cum_ones,
                 head_dim, n_tok):
    nb, n_pages = page_table.shape
    n_layers, n_pool, page, n_heads, _ = cache_k.shape
    width = n_heads * head_dim
    ck = cache_k.reshape(n_layers * n_pool, page, width)
    cv = cache_v.reshape(n_layers * n_pool, page, width)
    n_rows = n_heads * PAD_TOK
    base = layer * n_pool

    def page_idx(b, j, pt):
        return (base + pt[b * n_pages + (n_pages - 1 - j)], 0, 0)

    tok_blk = pl.BlockSpec((1, PAD_TOK, width), lambda b, j, pt: (b, 0, 0))
    grid_spec = pltpu.PrefetchScalarGridSpec(
        num_scalar_prefetch=1,
        grid=(nb, n_pages),
        in_specs=[tok_blk, tok_blk, tok_blk,
                  pl.BlockSpec((None, page, width), page_idx),
                  pl.BlockSpec((None, page, width), page_idx),
                  pl.BlockSpec((n_rows, LANES), lambda b, j, pt: (0, 0)),
                  pl.BlockSpec(cum_ones.shape, lambda b, j, pt: (0, 0))],
        out_specs=tok_blk,
        scratch_shapes=[pltpu.VMEM((n_rows, width), BF16),
                        pltpu.VMEM((n_rows, width), F32),
                        pltpu.VMEM((n_rows, LANES), F32)],
    )
    return pl.pallas_call(
        functools.partial(_attn_sample_kernel, head_dim, n_tok),
        grid_spec=grid_spec,
        out_shape=jax.ShapeDtypeStruct((nb, PAD_TOK, width), F32),
        compiler_params=_params(("parallel", "arbitrary")),
        name="attn_sample",
    )(page_table.reshape(-1), q_pad, kn_pad, vn_pad, ck, cv, bias_rows, cum_ones)


def _pad_lanes(x, width):
    return jnp.pad(x, [(0, 0)] * (x.ndim - 1) + [(0, width - x.shape[-1])])


def _pad_tokens(x, total):
    return jnp.pad(x, [(0, 0), (0, total - x.shape[1]), (0, 0)])


def _cum_ones(tk):
    r = jnp.arange(tk)[:, None]
    c = jnp.arange(tk)[None, :]
    return jnp.concatenate([(r > c), jnp.ones((tk, tk), bool)], axis=1).astype(BF16)


def kernel(x_prompt, x_sample, cache_k, cache_v, state_delta, state_conv, page_table, norm_mix, norm_mlp,
           ab_w_in, ab_w_out, gmlp_w_s, gmlp_b_s, gmlp_v_norm, gdn_conv_w, gdn_a_log, gdn_dt_bias,
           gdn_out_norm, sb_w_qkv, sb_w_out, sb_q_norm, sb_k_norm, sb_logit_bias, mlp_w_up, mlp_w_down):
    n_batch, seq, d_model = x_prompt.shape
    nb_s, n_tok, _ = x_sample.shape
    depth = norm_mix.shape[0]
    n_groups = gmlp_w_s.shape[1]
    a_width = n_groups * LANES
    n_bheads = state_delta.shape[2]
    b_qkv = state_conv.shape[-1]
    b_width = n_bheads * LANES
    c_heads, head_dim = cache_k.shape[3], cache_k.shape[4]
    n_prompt = n_batch * seq
    n_sample = nb_s * n_tok
    assert a_width == b_width and b_qkv == 3 * b_width and LANES % head_dim == 0
    assert seq % ROW_TILE == 0 and n_sample % ROW_TILE == 0 and n_sample % GMLP_CHUNK == 0
    assert GMLP_CHUNK % n_tok == 0 and n_tok <= PAD_TOK

    y_p = x_prompt.reshape(n_prompt, d_model)
    y_s = x_sample.reshape(n_sample, d_model)
    cum_ones = _cum_ones(ATT_BLOCK)
    seg = jnp.arange(LANES) // head_dim
    seg_ones = (seg[:, None] == seg[None, :]).astype(BF16)

    k_pr, v_pr, k_sa, v_sa = [], [], [], []
    d_pr, d_sa, c_pr, c_sa, gv_sa = [], [], [], [], []
    for layer in range(depth):
        li = layer // 2
        gain_mix = norm_mix[layer][None, :]
        if layer % 2 == 0:
            w_in = ab_w_in[li]
            o = 0
            a_u, o = w_in[:, o:o + a_width], o + a_width
            a_v, o = w_in[:, o:o + a_width], o + a_width
            qkv_w, o = w_in[:, o:o + b_qkv], o + b_qkv
            beta_w, o = w_in[:, o:o + n_bheads], o + n_bheads
            araw_w, o = w_in[:, o:o + n_bheads], o + n_bheads
            gate_w = w_in[:, o:]
            w_cat = jnp.concatenate([qkv_w, a_u, a_v, gate_w, _pad_lanes(beta_w, LANES),
                                     _pad_lanes(araw_w, LANES)], axis=1).astype(BF16)
            u_blk = b_qkv // a_width
            v_blk = u_blk + 1
            gate_blk = u_blk + 2
            sm_blk = (b_qkv + 3 * a_width) // (2 * LANES)
            tn = w_cat.shape[1] // 2
            w_out = ab_w_out[li].astype(BF16)
            alog = _pad_lanes(gdn_a_log[li][None, :], LANES)
            dtb = _pad_lanes(gdn_dt_bias[li][None, :], LANES)
            onorm = gdn_out_norm[li][None, :]
            conv_w = gdn_conv_w[li]
            v_norm = gmlp_v_norm[li][None, :]
            tril = jnp.tril(jnp.ones((GMLP_CHUNK, GMLP_CHUNK), bool))
            wmix_p = jnp.where(tril, gmlp_w_s[li], 0.0).astype(BF16)
            bias_p = jnp.broadcast_to(gmlp_b_s[li][:, :, None], (n_groups, GMLP_CHUNK, LANES))
            small = jnp.where(tril[:n_tok, :n_tok], gmlp_w_s[li][:, :n_tok, :n_tok], 0.0)
            eye_req = jnp.eye(GMLP_CHUNK // n_tok, dtype=F32)
            wmix_s = jnp.einsum('ab,gij->gaibj', eye_req, small).reshape(
                n_groups, GMLP_CHUNK, GMLP_CHUNK).astype(BF16)
            bias_s = jnp.broadcast_to(
                jnp.tile(gmlp_b_s[li][:, :n_tok], (1, GMLP_CHUNK // n_tok))[:, :, None],
                (n_groups, GMLP_CHUNK, LANES))

            proj_p = _norm_matmul(y_p, gain_mix, w_cat, ROW_TILE, tn)
            oa_p, _ = _gmlp(proj_p, u_blk, v_blk, v_norm, wmix_p, bias_p, ROW_TILE)
            ob_p, s_p = _gdn_prompt(proj_p, n_batch, seq, n_bheads, gate_blk, sm_blk, conv_w, alog, dtb, onorm)
            c_pr.append(proj_p[:, :b_qkv].reshape(n_batch, seq, b_qkv)[:, seq - (CONV_K - 1):])
            d_pr.append(s_p)
            y_p = _matmul_residual(y_p, [oa_p, ob_p], [w_out[:a_width], w_out[a_width:]], ROW_TILE)

            proj_s = _norm_matmul(y_s, gain_mix, w_cat, ROW_TILE, tn)
            oa_s, vn_s = _gmlp(proj_s, u_blk, v_blk, v_norm, wmix_s, bias_s, ROW_TILE)
            proj_s3 = proj_s.reshape(nb_s, n_tok, -1)
            qkv_s = proj_s3[:, :, :b_qkv]
            xp = _pad_tokens(jnp.concatenate([state_conv[li], qkv_s], axis=1), 2 * PAD_TOK)
            gate_s = _pad_tokens(proj_s3[:, :, gate_blk * a_width:(gate_blk + 1) * a_width], PAD_TOK)
            sm_s = _pad_tokens(proj_s3[:, :, sm_blk * 2 * LANES:], PAD_TOK)
            ob_s, s_s = _gdn_sample(xp, gate_s, sm_s, conv_w, alog, dtb, onorm, state_delta[li], n_tok, 8)
            ob_s = ob_s[:, :n_tok].reshape(n_sample, b_width)
            c_sa.append(qkv_s[:, n_tok - (CONV_K - 1):])
            d_sa.append(s_s)
            gv_sa.append(vn_s.reshape(nb_s, n_tok, a_width))
            y_s = _matmul_residual(y_s, [oa_s, ob_s], [w_out[:a_width], w_out[a_width:]], ROW_TILE)
        else:
            w_qkv = sb_w_qkv[li].astype(BF16)
            w_out = sb_w_out[li].astype(BF16)
            width = c_heads * head_dim
            q_gain = jnp.tile(sb_q_norm[li], LANES // head_dim)[None, :]
            k_gain = jnp.tile(sb_k_norm[li], LANES // head_dim)[None, :]
            bias = sb_logit_bias[li]

            qb, k_f, v_f, kb, vb = _sb_qkv(y_p, gain_mix, w_qkv, q_gain, k_gain, seg_ones, head_dim, ROW_TILE)
            o_p = _attn_prompt(qb, kb, vb, bias, cum_ones, n_batch, seq, head_dim)
            k_pr.append(k_f.reshape(n_batch, seq, c_heads, head_dim))
            v_pr.append(v_f.reshape(n_batch, seq, c_heads, head_dim))
            y_p = _matmul_residual(y_p, [o_p], [w_out], ROW_TILE)

            qb, k_f, v_f, _, _ = _sb_qkv(y_s, gain_mix, w_qkv, q_gain, k_gain, seg_ones, head_dim, ROW_TILE)
            q_pad = _pad_tokens(qb.astype(F32).reshape(nb_s, n_tok, width), PAD_TOK)
            kn_pad = _pad_tokens(k_f.reshape(nb_s, n_tok, width), PAD_TOK)
            vn_pad = _pad_tokens(v_f.reshape(nb_s, n_tok, width), PAD_TOK)
            bias_rows = jnp.broadcast_to(jnp.repeat(bias, PAD_TOK)[:, None], (c_heads * PAD_TOK, LANES))
            o_s = _attn_sample(page_table, q_pad, kn_pad, vn_pad, cache_k, cache_v, li, bias_rows, cum_ones,
                               head_dim, n_tok)
            o_s = o_s[:, :n_tok].reshape(n_sample, width)
            k_sa.append(k_f.reshape(nb_s, n_tok, c_heads, head_dim))
            v_sa.append(v_f.reshape(nb_s, n_tok, c_heads, head_dim))
            y_s = _matmul_residual(y_s, [o_s], [w_out], ROW_TILE)

        gain_mlp = norm_mlp[layer][None, :]
        w_up = mlp_w_up[layer].astype(BF16)
        w_down = mlp_w_down[layer].astype(BF16)
        y_p = _mlp(y_p, gain_mlp, w_up, w_down, ROW_TILE, 1024)
        y_s = _mlp(y_s, gain_mlp, w_up, w_down, ROW_TILE, 1024)

    return (y_p.reshape(n_batch, seq, d_model), y_s.reshape(nb_s, n_tok, d_model),
            jnp.stack(k_pr), jnp.stack(v_pr), jnp.stack(k_sa), jnp.stack(v_sa),
            jnp.stack(d_pr), jnp.stack(d_sa), jnp.stack(c_pr), jnp.stack(c_sa), jnp.stack(gv_sa))
```

```python
import functools

import jax
import jax.numpy as jnp
from jax import lax
from jax.experimental import pallas as pl
from jax.experimental.pallas import tpu as pltpu

F32 = jnp.float32
BF16 = jnp.bfloat16
HI = lax.Precision.HIGHEST
NORM_EPS = 1e-6
LANES = 128
SUBLANES = 8
VMEM_LIMIT_BYTES = 48 * 1024 * 1024

GMLP_CHUNK = 128
GDN_CHUNK = 64
CONV_K = 4
ROW_TILE = 512
ATT_BLOCK = 256
PAGE_BLOCK = 128
SAMPLE_PAGES_PER_STEP = 2
PAD_TOK = 8


def _params(semantics):
    return pltpu.CompilerParams(dimension_semantics=semantics, vmem_limit_bytes=VMEM_LIMIT_BYTES)


def _dot(a, b):
    return jnp.dot(a.astype(BF16), b.astype(BF16), preferred_element_type=F32)


def _dot_nt(a, b):
    return lax.dot_general(a.astype(BF16), b.astype(BF16), (((1,), (1,)), ((), ())),
                           preferred_element_type=F32)


def _dot_tn(a, b):
    return lax.dot_general(a.astype(BF16), b.astype(BF16), (((0,), (0,)), ((), ())),
                           preferred_element_type=F32)


def _dot32(a, b):
    return jnp.dot(a, b, precision=HI, preferred_element_type=F32)


def _dot32_nt(a, b):
    return lax.dot_general(a, b, (((1,), (1,)), ((), ())), precision=HI, preferred_element_type=F32)


def _split_dot(x, ones_mat):
    hi = x.astype(BF16)
    lo = (x - hi.astype(F32)).astype(BF16)
    return (jnp.dot(hi, ones_mat, preferred_element_type=F32)
            + jnp.dot(lo, ones_mat, preferred_element_type=F32))


def _sigmoid(x):
    return 1.0 / (1.0 + jnp.exp(-x))


def _silu(x):
    return x * _sigmoid(x)


def _softplus(x):
    return jnp.maximum(x, 0.0) + jnp.log1p(jnp.exp(-jnp.abs(x)))


def _gelu_tanh(x):
    return x * (0.5 * (1.0 + jnp.tanh(0.7978845608028654 * (x + 0.044715 * (x * x * x)))))


def _rms_rows(x, gain):
    ms = jnp.mean(x * x, axis=-1, keepdims=True)
    return x * lax.rsqrt(ms + NORM_EPS) * gain


def _norm_matmul_kernel(x_ref, g_ref, w_ref, o_ref, hn_ref):
    @pl.when(pl.program_id(1) == 0)
    def _():
        hn_ref[...] = _rms_rows(x_ref[...], g_ref[...]).astype(BF16)

    o_ref[...] = jnp.dot(hn_ref[...], w_ref[...], preferred_element_type=F32)


def _norm_matmul(x, gain, w, tm, tn):
    t, d = x.shape
    n = w.shape[1]
    return pl.pallas_call(
        _norm_matmul_kernel,
        grid=(t // tm, n // tn),
        in_specs=[pl.BlockSpec((tm, d), lambda i, j: (i, 0)),
                  pl.BlockSpec((1, d), lambda i, j: (0, 0)),
                  pl.BlockSpec((d, tn), lambda i, j: (0, j))],
        out_specs=pl.BlockSpec((tm, tn), lambda i, j: (i, j)),
        out_shape=jax.ShapeDtypeStruct((t, n), F32),
        scratch_shapes=[pltpu.VMEM((tm, d), BF16)],
        compiler_params=_params(("parallel", "arbitrary")),
        name="norm_matmul",
    )(x, gain, w)


def _matmul_residual_kernel(n_lhs, *refs):
    res_ref = refs[0]
    a_refs = refs[1:1 + n_lhs]
    w_refs = refs[1 + n_lhs:1 + 2 * n_lhs]
    o_ref = refs[1 + 2 * n_lhs]
    acc = res_ref[...]
    for a_ref, w_ref in zip(a_refs, w_refs):
        acc = acc + jnp.dot(a_ref[...].astype(BF16), w_ref[...], preferred_element_type=F32)
    o_ref[...] = acc


def _matmul_residual(res, lhs_list, w_list, tm):
    t, d = res.shape
    n_lhs = len(lhs_list)
    in_specs = [pl.BlockSpec((tm, d), lambda i: (i, 0))]
    in_specs += [pl.BlockSpec((tm, a.shape[1]), lambda i: (i, 0)) for a in lhs_list]
    in_specs += [pl.BlockSpec(w.shape, lambda i: (0, 0)) for w in w_list]
    return pl.pallas_call(
        functools.partial(_matmul_residual_kernel, n_lhs),
        grid=(t // tm,),
        in_specs=in_specs,
        out_specs=pl.BlockSpec((tm, d), lambda i: (i, 0)),
        out_shape=jax.ShapeDtypeStruct((t, d), F32),
        compiler_params=_params(("parallel",)),
        name="matmul_residual",
    )(res, *lhs_list, *w_list)


def _mlp_kernel(x_ref, g_ref, wu_ref, wd_ref, o_ref, hn_ref):
    f = pl.program_id(1)

    @pl.when(f == 0)
    def _():
        x = x_ref[...]
        hn_ref[...] = _rms_rows(x, g_ref[...]).astype(BF16)
        o_ref[...] = x

    h = jnp.dot(hn_ref[...], wu_ref[...], preferred_element_type=F32)
    h = jnp.maximum(h, 0.0)
    h = (h * h).astype(BF16)
    o_ref[...] += jnp.dot(h, wd_ref[...], preferred_element_type=F32)


def _mlp(x, gain, w_up, w_down, tm, tf):
    t, d = x.shape
    dff = w_up.shape[1]
    return pl.pallas_call(
        _mlp_kernel,
        grid=(t // tm, dff // tf),
        in_specs=[pl.BlockSpec((tm, d), lambda i, f: (i, 0)),
                  pl.BlockSpec((1, d), lambda i, f: (0, 0)),
                  pl.BlockSpec((d, tf), lambda i, f: (0, f)),
                  pl.BlockSpec((tf, d), lambda i, f: (f, 0))],
        out_specs=pl.BlockSpec((tm, d), lambda i, f: (i, 0)),
        out_shape=jax.ShapeDtypeStruct((t, d), F32),
        scratch_shapes=[pltpu.VMEM((tm, d), BF16)],
        compiler_params=_params(("parallel", "arbitrary")),
        name="mlp",
    )(x, gain, w_up, w_down)


def _gmlp_kernel(n_chunks, n_groups, u_ref, v_ref, vn_ref, wmix_ref, bias_ref, oa_ref, vout_ref):
    for c in range(n_chunks):
        rows = slice(c * GMLP_CHUNK, (c + 1) * GMLP_CHUNK)
        for g in range(n_groups):
            cols = slice(g * LANES, (g + 1) * LANES)
            vn = _rms_rows(_gelu_tanh(v_ref[rows, cols]), vn_ref[:, cols])
            vout_ref[rows, cols] = vn
            mixed = jnp.dot(wmix_ref[g], vn.astype(BF16), preferred_element_type=F32) + bias_ref[g]
            oa_ref[rows, cols] = _gelu_tanh(u_ref[rows, cols]) * mixed


def _gmlp(proj, u_blk, v_blk, v_norm, wmix, bias_full, rows_per_step):
    t = proj.shape[0]
    n_groups = wmix.shape[0]
    width = n_groups * LANES
    n_chunks = rows_per_step // GMLP_CHUNK
    return pl.pallas_call(
        functools.partial(_gmlp_kernel, n_chunks, n_groups),
        grid=(t // rows_per_step,),
        in_specs=[pl.BlockSpec((rows_per_step, width), lambda i: (i, u_blk)),
                  pl.BlockSpec((rows_per_step, width), lambda i: (i, v_blk)),
                  pl.BlockSpec((1, width), lambda i: (0, 0)),
                  pl.BlockSpec(wmix.shape, lambda i: (0, 0, 0)),
                  pl.BlockSpec(bias_full.shape, lambda i: (0, 0, 0))],
        out_specs=[pl.BlockSpec((rows_per_step, width), lambda i: (i, 0)),
                   pl.BlockSpec((rows_per_step, width), lambda i: (i, 0))],
        out_shape=[jax.ShapeDtypeStruct((t, width), F32), jax.ShapeDtypeStruct((t, width), F32)],
        compiler_params=_params(("parallel",)),
        name="gmlp",
    )(proj, proj, v_norm, wmix, bias_full)


def _l2norm_rows(x):
    return x * lax.rsqrt(jnp.sum(x * x, axis=-1, keepdims=True) + NORM_EPS)


def _gdn_gates(sm, alog, dtb):
    beta = _sigmoid(sm[:, :LANES])
    g = -jnp.exp(alog) * _softplus(sm[:, LANES:] + dtb)
    return beta, g


def _gdn_out(o, onorm, gate):
    return _rms_rows(o, onorm) * _silu(gate)


def _transpose_rows(x):
    rows = x.shape[0]
    if rows < LANES:
        x = jnp.concatenate([x, jnp.zeros((LANES - rows, LANES), F32)], axis=0)
    return jnp.transpose(x)


def _inv_unit_lower(lmat, size):
    r = lax.broadcasted_iota(jnp.int32, (size, size), 0)
    c = lax.broadcasted_iota(jnp.int32, (size, size), 1)
    x = -lmat
    p = jnp.where(r == c, 1.0, 0.0) + x
    n = 2
    while n < size:
        x = _dot32(x, x)
        p = p + _dot32(p, x)
        n *= 2
    return p


def _gdn_prompt_kernel(n_batch, n_heads, chunk, x_ref, gate_ref, sm_ref, cw_ref, alog_ref, dtb_ref, onorm_ref,
                       o_ref, s_out_ref, xbuf_ref, s_ref):
    c_idx = pl.program_id(0)
    hw = n_heads * LANES

    @pl.when(c_idx == 0)
    def _():
        xbuf_ref[:, 0:SUBLANES, :] = jnp.zeros((n_batch, SUBLANES, xbuf_ref.shape[2]), F32)
        s_ref[...] = jnp.zeros(s_ref.shape, F32)

    r = lax.broadcasted_iota(jnp.int32, (chunk, chunk), 0)
    c = lax.broadcasted_iota(jnp.int32, (chunk, chunk), 1)
    incl = c <= r
    strict = c < r
    incl_ones = jnp.where(incl, 1.0, 0.0)

    for b in range(n_batch):
        x = x_ref[b]
        xbuf_ref[b, SUBLANES:SUBLANES + chunk, :] = x
        conv = cw_ref[CONV_K - 1:CONV_K, :] * x
        for i in range(CONV_K - 1):
            shift = CONV_K - 1 - i
            conv = conv + cw_ref[i:i + 1, :] * xbuf_ref[b, SUBLANES - shift:SUBLANES - shift + chunk, :]
        xbuf_ref[b, 0:SUBLANES, :] = x[chunk - SUBLANES:chunk, :]
        act = _silu(conv)

        beta, g = _gdn_gates(sm_ref[b], alog_ref[...], dtb_ref[...])
        gc = _dot32(incl_ones, g)
        gc_t = _transpose_rows(gc)

        for h in range(n_heads):
            cols = slice(h * LANES, (h + 1) * LANES)
            q_h = _l2norm_rows(act[:, h * LANES:(h + 1) * LANES]) * (LANES ** -0.5)
            k_h = _l2norm_rows(act[:, hw + h * LANES:hw + (h + 1) * LANES])
            v_h = act[:, 2 * hw + h * LANES:2 * hw + (h + 1) * LANES]
            b_col = beta[:, h:h + 1]
            gc_col = gc[:, h:h + 1]
            gc_row = gc_t[h:h + 1, :chunk]
            decay = jnp.where(incl, jnp.exp(jnp.where(incl, gc_col - gc_row, 0.0)), 0.0)
            egc = jnp.exp(gc_col)
            g_last = gc[chunk - 1:chunk, h:h + 1]
            kb = k_h * b_col
            lmat = jnp.where(strict, _dot32_nt(kb, k_h) * decay, 0.0)
            tmat = _inv_unit_lower(lmat, chunk)
            u = _dot(tmat, v_h * b_col)
            w = _dot(tmat, kb * egc)
            a_in = _dot_nt(q_h, k_h) * decay
            q_dec = q_h * egc
            k_dec = k_h * jnp.exp(g_last - gc_col)
            s = s_ref[b, h]
            v_new = u - _dot(w, s)
            o = _dot(q_dec, s) + _dot(a_in, v_new)
            v_pad = jnp.concatenate([v_new, jnp.zeros((LANES - chunk, LANES), F32)], axis=0)
            s_ref[b, h] = s * jnp.exp(g_last) + _dot(_transpose_rows(k_dec), v_pad)
            o_ref[b, :, cols] = _gdn_out(o, onorm_ref[...], gate_ref[b, :, cols])

    @pl.when(c_idx == pl.num_programs(0) - 1)
    def _():
        s_out_ref[...] = s_ref[...]


def _gdn_prompt(proj, n_batch, seq, n_heads, gate_blk, sm_blk, conv_w, alog, dtb, onorm):
    chunk = GDN_CHUNK
    nc = seq // chunk
    qkv_w = 3 * n_heads * LANES
    hw = n_heads * LANES
    proj3 = proj.reshape(n_batch, seq, proj.shape[1])
    o, s_fin = pl.pallas_call(
        functools.partial(_gdn_prompt_kernel, n_batch, n_heads, chunk),
        grid=(nc,),
        in_specs=[pl.BlockSpec((n_batch, chunk, qkv_w), lambda c: (0, c, 0)),
                  pl.BlockSpec((n_batch, chunk, hw), lambda c: (0, c, gate_blk)),
                  pl.BlockSpec((n_batch, chunk, 2 * LANES), lambda c: (0, c, sm_blk)),
                  pl.BlockSpec((CONV_K, qkv_w), lambda c: (0, 0)),
                  pl.BlockSpec((1, LANES), lambda c: (0, 0)),
                  pl.BlockSpec((1, LANES), lambda c: (0, 0)),
                  pl.BlockSpec((1, LANES), lambda c: (0, 0))],
        out_specs=[pl.BlockSpec((n_batch, chunk, hw), lambda c: (0, c, 0)),
                   pl.BlockSpec((n_batch, n_heads, LANES, LANES), lambda c: (0, 0, 0, 0))],
        out_shape=[jax.ShapeDtypeStruct((n_batch, seq, hw), F32),
                   jax.ShapeDtypeStruct((n_batch, n_heads, LANES, LANES), F32)],
        scratch_shapes=[pltpu.VMEM((n_batch, SUBLANES + chunk, qkv_w), F32),
                        pltpu.VMEM((n_batch, n_heads, LANES, LANES), F32)],
        compiler_params=_params(("arbitrary",)),
        name="gdn_prompt",
    )(proj3, proj3, proj3, conv_w, alog, dtb, onorm)
    return o.reshape(n_batch * seq, hw), s_fin


def _gdn_sample_kernel(n_heads, n_tok, bb, xp_ref, gate_ref, sm_ref, cw_ref, alog_ref, dtb_ref, onorm_ref,
                       s0_ref, o_ref, s_out_ref):
    hw = n_heads * LANES

    def one_request(b, carry):
        conv = cw_ref[0:1, :] * xp_ref[b, 0:PAD_TOK, :]
        for i in range(1, CONV_K):
            conv = conv + cw_ref[i:i + 1, :] * xp_ref[b, i:i + PAD_TOK, :]
        act = _silu(conv)
        beta, g = _gdn_gates(sm_ref[b], alog_ref[...], dtb_ref[...])
        eg = jnp.exp(g)
        gate = gate_ref[b]
        ks = [_l2norm_rows(act[:, hw + h * LANES:hw + (h + 1) * LANES]) for h in range(n_heads)]
        qs = [_l2norm_rows(act[:, h * LANES:(h + 1) * LANES]) * (LANES ** -0.5) for h in range(n_heads)]
        kq_t = _transpose_rows(jnp.concatenate(ks + qs, axis=0))
        for h in range(n_heads):
            v_h = act[:, 2 * hw + h * LANES:2 * hw + (h + 1) * LANES]
            s = s0_ref[b, h]
            outs = []
            for t in range(n_tok):
                kc = kq_t[:, h * PAD_TOK + t:h * PAD_TOK + t + 1]
                qc = kq_t[:, (n_heads + h) * PAD_TOK + t:(n_heads + h) * PAD_TOK + t + 1]
                e = eg[t:t + 1, h:h + 1]
                ks = jnp.sum(s * kc, axis=0, keepdims=True)
                d = beta[t:t + 1, h:h + 1] * (v_h[t:t + 1, :] - e * ks)
                s = e * s + kc * d
                outs.append(jnp.sum(s * qc, axis=0, keepdims=True))
            outs.append(jnp.zeros((PAD_TOK - n_tok, LANES), F32))
            o = jnp.concatenate(outs, axis=0)
            s_out_ref[b, h] = s
            o_ref[b, :, h * LANES:(h + 1) * LANES] = _gdn_out(
                o, onorm_ref[...], gate[:, h * LANES:(h + 1) * LANES])
        return carry

    lax.fori_loop(0, bb, one_request, 0)


def _gdn_sample(xp, gate, sm, conv_w, alog, dtb, onorm, s0, n_tok, bb):
    nb, n_heads = s0.shape[0], s0.shape[1]
    hw = n_heads * LANES
    qkv_w = 3 * hw
    return pl.pallas_call(
        functools.partial(_gdn_sample_kernel, n_heads, n_tok, bb),
        grid=(nb // bb,),
        in_specs=[pl.BlockSpec((bb, 2 * PAD_TOK, qkv_w), lambda i: (i, 0, 0)),
                  pl.BlockSpec((bb, PAD_TOK, hw), lambda i: (i, 0, 0)),
                  pl.BlockSpec((bb, PAD_TOK, 2 * LANES), lambda i: (i, 0, 0)),
                  pl.BlockSpec((CONV_K, qkv_w), lambda i: (0, 0)),
                  pl.BlockSpec((1, LANES), lambda i: (0, 0)),
                  pl.BlockSpec((1, LANES), lambda i: (0, 0)),
                  pl.BlockSpec((1, LANES), lambda i: (0, 0)),
                  pl.BlockSpec((bb, n_heads, LANES, LANES), lambda i: (i, 0, 0, 0))],
        out_specs=[pl.BlockSpec((bb, PAD_TOK, hw), lambda i: (i, 0, 0)),
                   pl.BlockSpec((bb, n_heads, LANES, LANES), lambda i: (i, 0, 0, 0))],
        out_shape=[jax.ShapeDtypeStruct((nb, PAD_TOK, hw), F32),
                   jax.ShapeDtypeStruct(s0.shape, F32)],
        compiler_params=_params(("parallel",)),
        name="gdn_sample",
    )(xp, gate, sm, conv_w, alog, dtb, onorm, s0)


def _head_rms(r, gain, seg_ones, head_dim):
    ss = _split_dot(r * r, seg_ones)
    return r * lax.rsqrt(ss * (1.0 / head_dim) + NORM_EPS) * gain


def _sb_qkv_kernel(head_dim, x_ref, g_ref, w_ref, qn_ref, kn_ref, seg_ref,
                   qb_ref, k_ref, v_ref, kb_ref, vb_ref, hn_ref):
    j = pl.program_id(1)

    @pl.when(j == 0)
    def _():
        hn_ref[...] = _rms_rows(x_ref[...], g_ref[...]).astype(BF16)

    r = jnp.dot(hn_ref[...], w_ref[...], preferred_element_type=F32)
    n_blk = r.shape[1] // LANES
    seg = seg_ref[...]

    @pl.when(j == 0)
    def _():
        for c in range(n_blk):
            cols = slice(c * LANES, (c + 1) * LANES)
            qn = _head_rms(r[:, cols], qn_ref[...], seg, head_dim)
            qb_ref[:, cols] = (qn * (head_dim ** -0.5)).astype(BF16)

    @pl.when(j == 1)
    def _():
        for c in range(n_blk):
            cols = slice(c * LANES, (c + 1) * LANES)
            kn = _head_rms(r[:, cols], kn_ref[...], seg, head_dim)
            k_ref[:, cols] = kn
            kb_ref[:, cols] = kn.astype(BF16)

    @pl.when(j == 2)
    def _():
        v_ref[...] = r
        vb_ref[...] = r.astype(BF16)


def _sb_qkv(x, gain, w, q_gain, k_gain, seg_ones, head_dim, tm):
    t, d = x.shape
    width = w.shape[1] // 3
    row_blk = pl.BlockSpec((tm, width), lambda i, j: (i, 0))
    return pl.pallas_call(
        functools.partial(_sb_qkv_kernel, head_dim),
        grid=(t // tm, 3),
        in_specs=[pl.BlockSpec((tm, d), lambda i, j: (i, 0)),
                  pl.BlockSpec((1, d), lambda i, j: (0, 0)),
                  pl.BlockSpec((d, width), lambda i, j: (0, j)),
                  pl.BlockSpec((1, LANES), lambda i, j: (0, 0)),
                  pl.BlockSpec((1, LANES), lambda i, j: (0, 0)),
                  pl.BlockSpec((LANES, LANES), lambda i, j: (0, 0))],
        out_specs=[row_blk, row_blk, row_blk, row_blk, row_blk],
        out_shape=[jax.ShapeDtypeStruct((t, width), BF16),
                   jax.ShapeDtypeStruct((t, width), F32),
                   jax.ShapeDtypeStruct((t, width), F32),
                   jax.ShapeDtypeStruct((t, width), BF16),
                   jax.ShapeDtypeStruct((t, width), BF16)],
        scratch_shapes=[pltpu.VMEM((tm, d), BF16)],
        compiler_params=_params(("parallel", "arbitrary")),
        name="sb_qkv",
    )(x, gain, w, q_gain, k_gain, seg_ones)


def _stick_block(z, carry, cum_ones, mask):
    tk = z.shape[1]
    sp = _softplus(z)
    lk = -sp if mask is None else jnp.where(mask, -sp, 0.0)
    cs = _split_dot(lk, cum_ones)
    a = jnp.exp((z - sp) + cs[:, :tk] + carry)
    if mask is not None:
        a = jnp.where(mask, a, 0.0)
    return a, carry + cs[:, tk:]


def _attn_prompt_kernel(head_dim, bias_ref, q_ref, k_ref, v_ref, suf_ref, o_ref, acc_ref, carry_ref):
    p = pl.program_id(1)
    qi = pl.program_id(2)
    tq = q_ref.shape[0]
    tk = suf_ref.shape[0]
    rows = 2 * tq
    q = q_ref[...].astype(F32)
    lane = lax.broadcasted_iota(jnp.int32, (tq, LANES), 1)
    first = lane < head_dim
    qq = jnp.concatenate([jnp.where(first, q, 0.0), jnp.where(first, 0.0, q)], axis=0).astype(BF16)
    row1 = lax.broadcasted_iota(jnp.int32, (rows, 1), 0)
    bias = jnp.where(row1 < tq, bias_ref[2 * p], bias_ref[2 * p + 1])
    r2 = lax.broadcasted_iota(jnp.int32, (rows, tk), 0)
    c2 = lax.broadcasted_iota(jnp.int32, (rows, tk), 1)
    causal = c2 < jnp.where(r2 >= tq, r2 - tq, r2)
    acc_ref[...] = jnp.zeros(acc_ref.shape, F32)
    carry_ref[...] = jnp.zeros(carry_ref.shape, F32)

    def sub_block(j, masked):
        start = pl.multiple_of(j * tk, tk)
        kblk = k_ref[pl.ds(start, tk), :]
        vblk = v_ref[pl.ds(start, tk), :]
        z = lax.dot_general(qq, kblk, (((1,), (1,)), ((), ())), preferred_element_type=F32) + bias
        lk = jnp.minimum(-z, 0.0) - jnp.log(1.0 + jnp.exp(-jnp.abs(z)))
        if masked:
            lk = jnp.where(causal, lk, 0.0)
        cs = jnp.dot(lk.astype(BF16), suf_ref[...], preferred_element_type=F32)
        carry = carry_ref[...]
        a = jnp.exp(z + cs + jnp.concatenate([carry] * (tk // LANES), axis=1))
        if masked:
            a = jnp.where(causal, a, 0.0)
        acc_ref[...] += jnp.dot(a.astype(BF16), vblk, preferred_element_type=F32)
        carry_ref[...] = carry + jnp.sum(lk, axis=1, keepdims=True)

    sub_block(qi, True)

    def pair(t, c):
        j = qi - 1 - 2 * t
        sub_block(j, False)
        sub_block(j - 1, False)
        return c

    lax.fori_loop(0, lax.shift_right_logical(qi, 1), pair, 0)

    @pl.when((qi & 1) == 1)
    def _():
        sub_block(0, False)

    acc = acc_ref[...]
    o_ref[...] = jnp.where(first, acc[:tq], acc[tq:])


def _attn_prompt(qb, kb, vb, bias, suffix_ones, n_batch, seq, head_dim):
    width = qb.shape[1]
    tq = suffix_ones.shape[0]
    nq = seq // tq
    assert LANES == 2 * head_dim and seq % tq == 0
    return pl.pallas_call(
        functools.partial(_attn_prompt_kernel, head_dim),
        grid=(n_batch, width // LANES, nq),
        in_specs=[pl.BlockSpec(memory_space=pltpu.SMEM),
                  pl.BlockSpec((tq, LANES), lambda b, p, i: (b * nq + i, p)),
                  pl.BlockSpec((seq, LANES), lambda b, p, i: (b, p)),
                  pl.BlockSpec((seq, LANES), lambda b, p, i: (b, p)),
                  pl.BlockSpec(suffix_ones.shape, lambda b, p, i: (0, 0))],
        out_specs=pl.BlockSpec((tq, LANES), lambda b, p, i: (b * nq + i, p)),
        out_shape=jax.ShapeDtypeStruct((n_batch * seq, width), F32),
        scratch_shapes=[pltpu.VMEM((2 * tq, LANES), F32), pltpu.VMEM((2 * tq, LANES), F32)],
        compiler_params=_params(("parallel", "parallel", "arbitrary")),
        name="attn_prompt",
    )(bias, qb, kb, vb, suffix_ones)


def _attn_sample_kernel(head_dim, n_tok, n_kv, pt_ref, q_ref, kn_ref, vn_ref, *refs):
    kv_refs = refs[:n_kv]
    bias_ref, cum_ref, o_ref, qbd_ref, acc_ref, carry_ref = refs[n_kv:]
    j = pl.program_id(1)
    n_rows, width = qbd_ref.shape
    n_heads = width // head_dim
    row_head = lax.shift_right_logical(lax.broadcasted_iota(jnp.int32, (n_rows, width), 0),
                                       PAD_TOK.bit_length() - 1)
    col_head = lax.shift_right_logical(lax.broadcasted_iota(jnp.int32, (n_rows, width), 1),
                                       head_dim.bit_length() - 1)
    own = row_head == col_head
    bias = bias_ref[...]

    @pl.when(j == 0)
    def _():
        q8 = q_ref[0]
        qbd = jnp.where(own, jnp.concatenate([q8] * n_heads, axis=0), 0.0)
        qbd_ref[...] = qbd.astype(BF16)
        tok = lax.broadcasted_iota(jnp.int32, (n_rows, 1), 0) & (PAD_TOK - 1)
        bcol = bias[:, 0:1]
        lks, lss, valids = [], [], []
        for s in range(n_tok):
            z = jnp.sum(qbd * kn_ref[0, s:s + 1, :], axis=-1, keepdims=True) + bcol
            sp = _softplus(z)
            valid = tok > s
            valids.append(valid)
            lks.append(jnp.where(valid, -sp, 0.0))
            lss.append(z - sp)
        later = jnp.zeros((n_rows, 1), F32)
        acc = jnp.zeros((n_rows, width), F32)
        for s in reversed(range(n_tok)):
            a = jnp.where(valids[s], jnp.exp(lss[s] + later), 0.0)
            acc = acc + a * vn_ref[0, s:s + 1, :]
            later = later + lks[s]
        acc_ref[...] = acc
        carry_ref[...] = jnp.broadcast_to(later, carry_ref.shape)

    carry = carry_ref[...]
    acc = acc_ref[...]
    for k_ref, v_ref in zip(kv_refs[0::2], kv_refs[1::2]):
        z = jnp.dot(qbd_ref[...], k_ref[...].astype(BF16), preferred_element_type=F32) + bias
        a, carry = _stick_block(z, carry, cum_ref[...], None)
        acc = acc + lax.dot_general(a.astype(BF16), v_ref[...].astype(BF16), (((1,), (1,)), ((), ())),
                                    preferred_element_type=F32)
    carry_ref[...] = carry
    acc_ref[...] = acc

    @pl.when(j == pl.num_programs(1) - 1)
    def _():
        acc = jnp.where(own, acc_ref[...], 0.0)
        o = acc[0:PAD_TOK, :]
        for h in range(1, n_heads):
            o = o + acc[h * PAD_TOK:(h + 1) * PAD_TOK, :]
        o_ref[0] = o


def _attn_sample(page_table, q_pad, kn_pad, vn_pad, cache_k, cache_v, layer, bias_rows, cum_ones,
                 head_dim, n_tok):
    nb, n_pages = page_table.shape
    n_layers, n_pool, page, n_heads, _ = cache_k.shape
    width = n_heads * head_dim
    ck = jnp.transpose(cache_k, (0, 1, 3, 4, 2)).reshape(n_layers * n_pool, width, page)
    cv = jnp.transpose(cache_v, (0, 1, 3, 4, 2)).reshape(n_layers * n_pool, width, page)
    n_rows = n_heads * PAD_TOK
    base = layer * n_pool
    per_step = SAMPLE_PAGES_PER_STEP
    assert n_pages % per_step == 0

    def page_spec(g):
        def page_idx(b, j, pt):
            return (base + pt[b * n_pages + (n_pages - 1 - (j * per_step + g))], 0, 0)
        return pl.BlockSpec((None, width, page), page_idx)

    kv_specs, kv_args = [], []
    for g in range(per_step):
        kv_specs += [page_spec(g), page_spec(g)]
        kv_args += [ck, cv]
    tok_blk = pl.BlockSpec((1, PAD_TOK, width), lambda b, j, pt: (b, 0, 0))
    grid_spec = pltpu.PrefetchScalarGridSpec(
        num_scalar_prefetch=1,
        grid=(nb, n_pages // per_step),
        in_specs=[tok_blk, tok_blk, tok_blk] + kv_specs + [
            pl.BlockSpec((n_rows, LANES), lambda b, j, pt: (0, 0)),
            pl.BlockSpec(cum_ones.shape, lambda b, j, pt: (0, 0))],
        out_specs=tok_blk,
        scratch_shapes=[pltpu.VMEM((n_rows, width), BF16),
                        pltpu.VMEM((n_rows, width), F32),
                        pltpu.VMEM((n_rows, LANES), F32)],
    )
    return pl.pallas_call(
        functools.partial(_attn_sample_kernel, head_dim, n_tok, 2 * per_step),
        grid_spec=grid_spec,
        out_shape=jax.ShapeDtypeStruct((nb, PAD_TOK, width), F32),
        compiler_params=_params(("parallel", "arbitrary")),
        name="attn_sample",
    )(page_table.reshape(-1), q_pad, kn_pad, vn_pad, *kv_args, bias_rows, cum_ones)


def _pad_lanes(x, width):
    return jnp.pad(x, [(0, 0)] * (x.ndim - 1) + [(0, width - x.shape[-1])])


def _pad_tokens(x, total):
    return jnp.pad(x, [(0, 0), (0, total - x.shape[1]), (0, 0)])


def _cum_ones(tk):
    r = jnp.arange(tk)[:, None]
    c = jnp.arange(tk)[None, :]
    return jnp.concatenate([(r > c), jnp.ones((tk, tk), bool)], axis=1).astype(BF16)


def kernel(x_prompt, x_sample, cache_k, cache_v, state_delta, state_conv, page_table, norm_mix, norm_mlp,
           ab_w_in, ab_w_out, gmlp_w_s, gmlp_b_s, gmlp_v_norm, gdn_conv_w, gdn_a_log, gdn_dt_bias,
           gdn_out_norm, sb_w_qkv, sb_w_out, sb_q_norm, sb_k_norm, sb_logit_bias, mlp_w_up, mlp_w_down):
    n_batch, seq, d_model = x_prompt.shape
    nb_s, n_tok, _ = x_sample.shape
    depth = norm_mix.shape[0]
    n_groups = gmlp_w_s.shape[1]
    a_width = n_groups * LANES
    n_bheads = state_delta.shape[2]
    b_qkv = state_conv.shape[-1]
    b_width = n_bheads * LANES
    c_heads, head_dim = cache_k.shape[3], cache_k.shape[4]
    n_prompt = n_batch * seq
    n_sample = nb_s * n_tok
    assert a_width == b_width and b_qkv == 3 * b_width and LANES % head_dim == 0
    assert seq % ROW_TILE == 0 and n_sample % ROW_TILE == 0 and n_sample % GMLP_CHUNK == 0
    assert GMLP_CHUNK % n_tok == 0 and n_tok <= PAD_TOK

    y_p = x_prompt.reshape(n_prompt, d_model)
    y_s = x_sample.reshape(n_sample, d_model)
    assert cache_k.shape[2] == PAGE_BLOCK
    cum_ones = _cum_ones(PAGE_BLOCK)
    blk = jnp.arange(min(ATT_BLOCK, seq))
    suffix_ones = (blk[:, None] >= blk[None, :]).astype(BF16)
    seg = jnp.arange(LANES) // head_dim
    seg_ones = (seg[:, None] == seg[None, :]).astype(BF16)

    k_pr, v_pr, k_sa, v_sa = [], [], [], []
    d_pr, d_sa, c_pr, c_sa, gv_sa = [], [], [], [], []
    for layer in range(depth):
        li = layer // 2
        gain_mix = norm_mix[layer][None, :]
        if layer % 2 == 0:
            w_in = ab_w_in[li]
            o = 0
            a_u, o = w_in[:, o:o + a_width], o + a_width
            a_v, o = w_in[:, o:o + a_width], o + a_width
            qkv_w, o = w_in[:, o:o + b_qkv], o + b_qkv
            beta_w, o = w_in[:, o:o + n_bheads], o + n_bheads
            araw_w, o = w_in[:, o:o + n_bheads], o + n_bheads
            gate_w = w_in[:, o:]
            w_cat = jnp.concatenate([qkv_w, a_u, a_v, gate_w, _pad_lanes(beta_w, LANES),
                                     _pad_lanes(araw_w, LANES)], axis=1).astype(BF16)
            u_blk = b_qkv // a_width
            v_blk = u_blk + 1
            gate_blk = u_blk + 2
            sm_blk = (b_qkv + 3 * a_width) // (2 * LANES)
            tn = w_cat.shape[1] // 2
            w_out = ab_w_out[li].astype(BF16)
            alog = _pad_lanes(gdn_a_log[li][None, :], LANES)
            dtb = _pad_lanes(gdn_dt_bias[li][None, :], LANES)
            onorm = gdn_out_norm[li][None, :]
            conv_w = gdn_conv_w[li]
            v_norm = gmlp_v_norm[li][None, :]
            tril = jnp.tril(jnp.ones((GMLP_CHUNK, GMLP_CHUNK), bool))
            wmix_p = jnp.where(tril, gmlp_w_s[li], 0.0).astype(BF16)
            bias_p = jnp.broadcast_to(gmlp_b_s[li][:, :, None], (n_groups, GMLP_CHUNK, LANES))
            small = jnp.where(tril[:n_tok, :n_tok], gmlp_w_s[li][:, :n_tok, :n_tok], 0.0)
            eye_req = jnp.eye(GMLP_CHUNK // n_tok, dtype=F32)
            wmix_s = jnp.einsum('ab,gij->gaibj', eye_req, small).reshape(
                n_groups, GMLP_CHUNK, GMLP_CHUNK).astype(BF16)
            bias_s = jnp.broadcast_to(
                jnp.tile(gmlp_b_s[li][:, :n_tok], (1, GMLP_CHUNK // n_tok))[:, :, None],
                (n_groups, GMLP_CHUNK, LANES))

            proj_p = _norm_matmul(y_p, gain_mix, w_cat, ROW_TILE, tn)
            oa_p, _ = _gmlp(proj_p, u_blk, v_blk, v_norm, wmix_p, bias_p, ROW_TILE)
            ob_p, s_p = _gdn_prompt(proj_p, n_batch, seq, n_bheads, gate_blk, sm_blk, conv_w, alog, dtb, onorm)
            c_pr.append(proj_p[:, :b_qkv].reshape(n_batch, seq, b_qkv)[:, seq - (CONV_K - 1):])
            d_pr.append(s_p)
            y_p = _matmul_residual(y_p, [oa_p, ob_p], [w_out[:a_width], w_out[a_width:]], ROW_TILE)

            proj_s = _norm_matmul(y_s, gain_mix, w_cat, ROW_TILE, tn)
            oa_s, vn_s = _gmlp(proj_s, u_blk, v_blk, v_norm, wmix_s, bias_s, ROW_TILE)
            proj_s3 = proj_s.reshape(nb_s, n_tok, -1)
            qkv_s = proj_s3[:, :, :b_qkv]
            xp = _pad_tokens(jnp.concatenate([state_conv[li], qkv_s], axis=1), 2 * PAD_TOK)
            gate_s = _pad_tokens(proj_s3[:, :, gate_blk * a_width:(gate_blk + 1) * a_width], PAD_TOK)
            sm_s = _pad_tokens(proj_s3[:, :, sm_blk * 2 * LANES:], PAD_TOK)
            ob_s, s_s = _gdn_sample(xp, gate_s, sm_s, conv_w, alog, dtb, onorm, state_delta[li], n_tok, 8)
            ob_s = ob_s[:, :n_tok].reshape(n_sample, b_width)
            c_sa.append(qkv_s[:, n_tok - (CONV_K - 1):])
            d_sa.append(s_s)
            gv_sa.append(vn_s.reshape(nb_s, n_tok, a_width))
            y_s = _matmul_residual(y_s, [oa_s, ob_s], [w_out[:a_width], w_out[a_width:]], ROW_TILE)
        else:
            w_qkv = sb_w_qkv[li].astype(BF16)
            w_out = sb_w_out[li].astype(BF16)
            width = c_heads * head_dim
            q_gain = jnp.tile(sb_q_norm[li], LANES // head_dim)[None, :]
            k_gain = jnp.tile(sb_k_norm[li], LANES // head_dim)[None, :]
            bias = sb_logit_bias[li]

            qb, k_f, v_f, kb, vb = _sb_qkv(y_p, gain_mix, w_qkv, q_gain, k_gain, seg_ones, head_dim, ROW_TILE)
            o_p = _attn_prompt(qb, kb, vb, bias, suffix_ones, n_batch, seq, head_dim)
            k_pr.append(k_f.reshape(n_batch, seq, c_heads, head_dim))
            v_pr.append(v_f.reshape(n_batch, seq, c_heads, head_dim))
            y_p = _matmul_residual(y_p, [o_p], [w_out], ROW_TILE)

            qb, k_f, v_f, _, _ = _sb_qkv(y_s, gain_mix, w_qkv, q_gain, k_gain, seg_ones, head_dim, ROW_TILE)
            q_pad = _pad_tokens(qb.astype(F32).reshape(nb_s, n_tok, width), PAD_TOK)
            kn_pad = _pad_tokens(k_f.reshape(nb_s, n_tok, width), PAD_TOK)
            vn_pad = _pad_tokens(v_f.reshape(nb_s, n_tok, width), PAD_TOK)
            bias_rows = jnp.broadcast_to(jnp.repeat(bias, PAD_TOK)[:, None], (c_heads * PAD_TOK, LANES))
            o_s = _attn_sample(page_table, q_pad, kn_pad, vn_pad, cache_k, cache_v, li, bias_rows, cum_ones,
                               head_dim, n_tok)
            o_s = o_s[:, :n_tok].reshape(n_sample, width)
            k_sa.append(k_f.reshape(nb_s, n_tok, c_heads, head_dim))
            v_sa.append(v_f.reshape(nb_s, n_tok, c_heads, head_dim))
            y_s = _matmul_residual(y_s, [o_s], [w_out], ROW_TILE)

        gain_mlp = norm_mlp[layer][None, :]
        w_up = mlp_w_up[layer].astype(BF16)
        w_down = mlp_w_down[layer].astype(BF16)
        y_p = _mlp(y_p, gain_mlp, w_up, w_down, ROW_TILE, 1024)
        y_s = _mlp(y_s, gain_mlp, w_up, w_down, ROW_TILE, 1024)

    return (y_p.reshape(n_batch, seq, d_model), y_s.reshape(nb_s, n_tok, d_model),
            jnp.stack(k_pr), jnp.stack(v_pr), jnp.stack(k_sa), jnp.stack(v_sa),
            jnp.stack(d_pr), jnp.stack(d_sa), jnp.stack(c_pr), jnp.stack(c_sa), jnp.stack(gv_sa))
```

```python
import functools

import jax
import jax.numpy as jnp
from jax import lax
from jax.experimental import pallas as pl
from jax.experimental.pallas import tpu as pltpu

F32 = jnp.float32
BF16 = jnp.bfloat16
HI = lax.Precision.HIGHEST
NORM_EPS = 1e-6
LOG2E = 1.4426950408889634
LANES = 128
SUBLANES = 8
VMEM_LIMIT_BYTES = 48 * 1024 * 1024

GMLP_CHUNK = 128
GDN_CHUNK = 64
GDN_CHUNKS_PER_STEP = 2
CONV_K = 4
ROW_TILE = 512
ATT_BLOCK = 256
PAGE_BLOCK = 128
SAMPLE_PAGES_PER_STEP = 2
PAD_TOK = 8


def _params(semantics):
    return pltpu.CompilerParams(dimension_semantics=semantics, vmem_limit_bytes=VMEM_LIMIT_BYTES)


def _dot(a, b):
    return jnp.dot(a.astype(BF16), b.astype(BF16), preferred_element_type=F32)


def _dot_nt(a, b):
    return lax.dot_general(a.astype(BF16), b.astype(BF16), (((1,), (1,)), ((), ())),
                           preferred_element_type=F32)


def _dot_tn(a, b):
    return lax.dot_general(a.astype(BF16), b.astype(BF16), (((0,), (0,)), ((), ())),
                           preferred_element_type=F32)


def _dot32(a, b):
    return jnp.dot(a, b, precision=HI, preferred_element_type=F32)


def _dot32_nt(a, b):
    return lax.dot_general(a, b, (((1,), (1,)), ((), ())), precision=HI, preferred_element_type=F32)


def _split_dot(x, ones_mat):
    hi = x.astype(BF16)
    lo = (x - hi.astype(F32)).astype(BF16)
    return (jnp.dot(hi, ones_mat, preferred_element_type=F32)
            + jnp.dot(lo, ones_mat, preferred_element_type=F32))


def _sigmoid(x):
    return 1.0 / (1.0 + jnp.exp(-x))


def _silu(x):
    return x * _sigmoid(x)


def _softplus(x):
    return jnp.maximum(x, 0.0) + jnp.log1p(jnp.exp(-jnp.abs(x)))


def _gelu_tanh(x):
    return x * (0.5 * (1.0 + jnp.tanh(0.7978845608028654 * (x + 0.044715 * (x * x * x)))))


def _rms_rows(x, gain):
    ms = jnp.mean(x * x, axis=-1, keepdims=True)
    return x * lax.rsqrt(ms + NORM_EPS) * gain


def _norm_matmul_kernel(x_ref, g_ref, w_ref, o_ref, hn_ref):
    @pl.when(pl.program_id(1) == 0)
    def _():
        hn_ref[...] = _rms_rows(x_ref[...], g_ref[...]).astype(BF16)

    o_ref[...] = jnp.dot(hn_ref[...], w_ref[...], preferred_element_type=F32)


def _norm_matmul(x, gain, w, tm, tn):
    t, d = x.shape
    n = w.shape[1]
    return pl.pallas_call(
        _norm_matmul_kernel,
        grid=(t // tm, n // tn),
        in_specs=[pl.BlockSpec((tm, d), lambda i, j: (i, 0)),
                  pl.BlockSpec((1, d), lambda i, j: (0, 0)),
                  pl.BlockSpec((d, tn), lambda i, j: (0, j))],
        out_specs=pl.BlockSpec((tm, tn), lambda i, j: (i, j)),
        out_shape=jax.ShapeDtypeStruct((t, n), F32),
        scratch_shapes=[pltpu.VMEM((tm, d), BF16)],
        compiler_params=_params(("parallel", "arbitrary")),
        name="norm_matmul",
    )(x, gain, w)


def _matmul_residual_kernel(n_lhs, *refs):
    res_ref = refs[0]
    a_refs = refs[1:1 + n_lhs]
    w_refs = refs[1 + n_lhs:1 + 2 * n_lhs]
    o_ref = refs[1 + 2 * n_lhs]
    acc = res_ref[...]
    for a_ref, w_ref in zip(a_refs, w_refs):
        acc = acc + jnp.dot(a_ref[...].astype(BF16), w_ref[...], preferred_element_type=F32)
    o_ref[...] = acc


def _matmul_residual(res, lhs_list, w_list, tm):
    t, d = res.shape
    n_lhs = len(lhs_list)
    in_specs = [pl.BlockSpec((tm, d), lambda i: (i, 0))]
    in_specs += [pl.BlockSpec((tm, a.shape[1]), lambda i: (i, 0)) for a in lhs_list]
    in_specs += [pl.BlockSpec(w.shape, lambda i: (0, 0)) for w in w_list]
    return pl.pallas_call(
        functools.partial(_matmul_residual_kernel, n_lhs),
        grid=(t // tm,),
        in_specs=in_specs,
        out_specs=pl.BlockSpec((tm, d), lambda i: (i, 0)),
        out_shape=jax.ShapeDtypeStruct((t, d), F32),
        compiler_params=_params(("parallel",)),
        name="matmul_residual",
    )(res, *lhs_list, *w_list)


def _mlp_kernel(x_ref, g_ref, wu_ref, wd_ref, o_ref, hn_ref):
    f = pl.program_id(1)

    @pl.when(f == 0)
    def _():
        x = x_ref[...]
        hn_ref[...] = _rms_rows(x, g_ref[...]).astype(BF16)
        o_ref[...] = x

    h = jnp.dot(hn_ref[...], wu_ref[...], preferred_element_type=F32)
    h = jnp.maximum(h, 0.0)
    h = (h * h).astype(BF16)
    o_ref[...] += jnp.dot(h, wd_ref[...], preferred_element_type=F32)


def _mlp(x, gain, w_up, w_down, tm, tf):
    t, d = x.shape
    dff = w_up.shape[1]
    return pl.pallas_call(
        _mlp_kernel,
        grid=(t // tm, dff // tf),
        in_specs=[pl.BlockSpec((tm, d), lambda i, f: (i, 0)),
                  pl.BlockSpec((1, d), lambda i, f: (0, 0)),
                  pl.BlockSpec((d, tf), lambda i, f: (0, f)),
                  pl.BlockSpec((tf, d), lambda i, f: (f, 0))],
        out_specs=pl.BlockSpec((tm, d), lambda i, f: (i, 0)),
        out_shape=jax.ShapeDtypeStruct((t, d), F32),
        scratch_shapes=[pltpu.VMEM((tm, d), BF16)],
        compiler_params=_params(("parallel", "arbitrary")),
        name="mlp",
    )(x, gain, w_up, w_down)


def _gmlp_kernel(n_chunks, n_groups, u_ref, v_ref, vn_ref, wmix_ref, bias_ref, oa_ref, vout_ref):
    for c in range(n_chunks):
        rows = slice(c * GMLP_CHUNK, (c + 1) * GMLP_CHUNK)
        for g in range(n_groups):
            cols = slice(g * LANES, (g + 1) * LANES)
            vn = _rms_rows(_gelu_tanh(v_ref[rows, cols]), vn_ref[:, cols])
            vout_ref[rows, cols] = vn
            mixed = jnp.dot(wmix_ref[g], vn.astype(BF16), preferred_element_type=F32) + bias_ref[g]
            oa_ref[rows, cols] = _gelu_tanh(u_ref[rows, cols]) * mixed


def _gmlp(proj, u_blk, v_blk, v_norm, wmix, bias_full, rows_per_step):
    t = proj.shape[0]
    n_groups = wmix.shape[0]
    width = n_groups * LANES
    n_chunks = rows_per_step // GMLP_CHUNK
    return pl.pallas_call(
        functools.partial(_gmlp_kernel, n_chunks, n_groups),
        grid=(t // rows_per_step,),
        in_specs=[pl.BlockSpec((rows_per_step, width), lambda i: (i, u_blk)),
                  pl.BlockSpec((rows_per_step, width), lambda i: (i, v_blk)),
                  pl.BlockSpec((1, width), lambda i: (0, 0)),
                  pl.BlockSpec(wmix.shape, lambda i: (0, 0, 0)),
                  pl.BlockSpec(bias_full.shape, lambda i: (0, 0, 0))],
        out_specs=[pl.BlockSpec((rows_per_step, width), lambda i: (i, 0)),
                   pl.BlockSpec((rows_per_step, width), lambda i: (i, 0))],
        out_shape=[jax.ShapeDtypeStruct((t, width), F32), jax.ShapeDtypeStruct((t, width), F32)],
        compiler_params=_params(("parallel",)),
        name="gmlp",
    )(proj, proj, v_norm, wmix, bias_full)


def _l2norm_rows(x):
    return x * lax.rsqrt(jnp.sum(x * x, axis=-1, keepdims=True) + NORM_EPS)


def _gdn_gates(sm, alog, dtb):
    beta = _sigmoid(sm[:, :LANES])
    g = -jnp.exp(alog) * _softplus(sm[:, LANES:] + dtb)
    return beta, g


def _gdn_out(o, onorm, gate):
    return _rms_rows(o, onorm) * _silu(gate)


def _transpose_rows(x):
    rows = x.shape[0]
    if rows < LANES:
        x = jnp.concatenate([x, jnp.zeros((LANES - rows, LANES), F32)], axis=0)
    return jnp.transpose(x)


def _split3(x):
    hi = x.astype(BF16)
    return hi, (x - hi.astype(F32)).astype(BF16)


def _dot3(a, b, nt=False):
    dims = (((1,), (1,)), ((), ())) if nt else (((1,), (0,)), ((), ()))
    mm = lambda u, v: lax.dot_general(u, v, dims, preferred_element_type=F32)
    return mm(a[0], b[0]) + (mm(a[0], b[1]) + mm(a[1], b[0]))


def _inv_unit_lower(lmat, size):
    r = lax.broadcasted_iota(jnp.int32, (size, size), 0)
    c = lax.broadcasted_iota(jnp.int32, (size, size), 1)
    x = _split3(-lmat)
    p = jnp.where(r == c, 1.0, 0.0) - lmat
    n = 2
    while n < size:
        x = _split3(_dot3(x, x))
        p = p + _dot3(_split3(p), x)
        n *= 2
    return p


def _gdn_prompt_kernel(n_batch, n_heads, chunk, x_ref, gate_ref, sm_ref, cw_ref, alog_ref, dtb_ref, onorm_ref,
                       o_ref, s_out_ref, xbuf_ref, s_ref):
    c_idx = pl.program_id(0)
    hw = n_heads * LANES
    step_rows = x_ref.shape[1]
    n_chunks = step_rows // chunk

    @pl.when(c_idx == 0)
    def _():
        xbuf_ref[:, 0:SUBLANES, :] = jnp.zeros((n_batch, SUBLANES, xbuf_ref.shape[2]), F32)
        s_ref[...] = jnp.zeros(s_ref.shape, F32)

    r = lax.broadcasted_iota(jnp.int32, (chunk, chunk), 0)
    c = lax.broadcasted_iota(jnp.int32, (chunk, chunk), 1)
    incl = c <= r
    strict = c < r
    incl_ones = jnp.where(incl, 1.0, 0.0)

    for b in range(n_batch):
        x = x_ref[b]
        xbuf_ref[b, SUBLANES:SUBLANES + step_rows, :] = x
        conv = cw_ref[CONV_K - 1:CONV_K, :] * x
        for i in range(CONV_K - 1):
            shift = CONV_K - 1 - i
            conv = conv + cw_ref[i:i + 1, :] * xbuf_ref[b, SUBLANES - shift:SUBLANES - shift + step_rows, :]
        xbuf_ref[b, 0:SUBLANES, :] = x[step_rows - SUBLANES:step_rows, :]
        act_all = _silu(conv)
        beta_all, g_all = _gdn_gates(sm_ref[b], alog_ref[...], dtb_ref[...])

        for cc in range(n_chunks):
            rows = slice(cc * chunk, (cc + 1) * chunk)
            act = act_all[rows]
            beta = beta_all[rows]
            gc = _dot32(incl_ones, g_all[rows])
            gc_t = _transpose_rows(gc)

            for h in range(n_heads):
                cols = slice(h * LANES, (h + 1) * LANES)
                q_h = _l2norm_rows(act[:, h * LANES:(h + 1) * LANES]) * (LANES ** -0.5)
                k_h = _l2norm_rows(act[:, hw + h * LANES:hw + (h + 1) * LANES])
                v_h = act[:, 2 * hw + h * LANES:2 * hw + (h + 1) * LANES]
                b_col = beta[:, h:h + 1]
                gc_col = gc[:, h:h + 1]
                gc_row = gc_t[h:h + 1, :chunk]
                decay = jnp.where(incl, jnp.exp(jnp.where(incl, gc_col - gc_row, 0.0)), 0.0)
                egc = jnp.exp(gc_col)
                g_last = gc[chunk - 1:chunk, h:h + 1]
                kb = k_h * b_col
                lmat = jnp.where(strict, _dot3(_split3(kb), _split3(k_h), nt=True) * decay, 0.0)
                tmat = _inv_unit_lower(lmat, chunk)
                u = _dot(tmat, v_h * b_col)
                w = _dot(tmat, kb * egc)
                a_in = _dot_nt(q_h, k_h) * decay
                q_dec = q_h * egc
                k_dec = k_h * jnp.exp(g_last - gc_col)
                s = s_ref[b, h]
                v_new = u - _dot(w, s)
                o = _dot(q_dec, s) + _dot(a_in, v_new)
                v_pad = jnp.concatenate([v_new, jnp.zeros((LANES - chunk, LANES), F32)], axis=0)
                s_ref[b, h] = s * jnp.exp(g_last) + _dot(_transpose_rows(k_dec), v_pad)
                o_ref[b, rows, cols] = _gdn_out(o, onorm_ref[...], gate_ref[b, rows, cols])

    @pl.when(c_idx == pl.num_programs(0) - 1)
    def _():
        s_out_ref[...] = s_ref[...]


def _gdn_prompt(proj, n_batch, seq, n_heads, gate_blk, sm_blk, conv_w, alog, dtb, onorm):
    chunk = GDN_CHUNK
    step = min(GDN_CHUNKS_PER_STEP * chunk, seq)
    assert seq % step == 0 and step % chunk == 0
    qkv_w = 3 * n_heads * LANES
    hw = n_heads * LANES
    proj3 = proj.reshape(n_batch, seq, proj.shape[1])
    o, s_fin = pl.pallas_call(
        functools.partial(_gdn_prompt_kernel, n_batch, n_heads, chunk),
        grid=(seq // step,),
        in_specs=[pl.BlockSpec((n_batch, step, qkv_w), lambda c: (0, c, 0)),
                  pl.BlockSpec((n_batch, step, hw), lambda c: (0, c, gate_blk)),
                  pl.BlockSpec((n_batch, step, 2 * LANES), lambda c: (0, c, sm_blk)),
                  pl.BlockSpec((CONV_K, qkv_w), lambda c: (0, 0)),
                  pl.BlockSpec((1, LANES), lambda c: (0, 0)),
                  pl.BlockSpec((1, LANES), lambda c: (0, 0)),
                  pl.BlockSpec((1, LANES), lambda c: (0, 0))],
        out_specs=[pl.BlockSpec((n_batch, step, hw), lambda c: (0, c, 0)),
                   pl.BlockSpec((n_batch, n_heads, LANES, LANES), lambda c: (0, 0, 0, 0))],
        out_shape=[jax.ShapeDtypeStruct((n_batch, seq, hw), F32),
                   jax.ShapeDtypeStruct((n_batch, n_heads, LANES, LANES), F32)],
        scratch_shapes=[pltpu.VMEM((n_batch, SUBLANES + step, qkv_w), F32),
                        pltpu.VMEM((n_batch, n_heads, LANES, LANES), F32)],
        compiler_params=_params(("arbitrary",)),
        name="gdn_prompt",
    )(proj3, proj3, proj3, conv_w, alog, dtb, onorm)
    return o.reshape(n_batch * seq, hw), s_fin


def _gdn_sample_kernel(n_heads, n_tok, bb, xp_ref, gate_ref, sm_ref, cw_ref, alog_ref, dtb_ref, onorm_ref,
                       s0_ref, o_ref, s_out_ref):
    hw = n_heads * LANES

    def one_request(b, carry):
        conv = cw_ref[0:1, :] * xp_ref[b, 0:PAD_TOK, :]
        for i in range(1, CONV_K):
            conv = conv + cw_ref[i:i + 1, :] * xp_ref[b, i:i + PAD_TOK, :]
        act = _silu(conv)
        beta, g = _gdn_gates(sm_ref[b], alog_ref[...], dtb_ref[...])
        eg = jnp.exp(g)
        gate = gate_ref[b]
        ks = [_l2norm_rows(act[:, hw + h * LANES:hw + (h + 1) * LANES]) for h in range(n_heads)]
        qs = [_l2norm_rows(act[:, h * LANES:(h + 1) * LANES]) * (LANES ** -0.5) for h in range(n_heads)]
        kq_t = _transpose_rows(jnp.concatenate(ks + qs, axis=0))
        for h in range(n_heads):
            v_h = act[:, 2 * hw + h * LANES:2 * hw + (h + 1) * LANES]
            s = s0_ref[b, h]
            outs = []
            for t in range(n_tok):
                kc = kq_t[:, h * PAD_TOK + t:h * PAD_TOK + t + 1]
                qc = kq_t[:, (n_heads + h) * PAD_TOK + t:(n_heads + h) * PAD_TOK + t + 1]
                e = eg[t:t + 1, h:h + 1]
                ks = jnp.sum(s * kc, axis=0, keepdims=True)
                d = beta[t:t + 1, h:h + 1] * (v_h[t:t + 1, :] - e * ks)
                s = e * s + kc * d
                outs.append(jnp.sum(s * qc, axis=0, keepdims=True))
            outs.append(jnp.zeros((PAD_TOK - n_tok, LANES), F32))
            o = jnp.concatenate(outs, axis=0)
            s_out_ref[b, h] = s
            o_ref[b, :, h * LANES:(h + 1) * LANES] = _gdn_out(
                o, onorm_ref[...], gate[:, h * LANES:(h + 1) * LANES])
        return carry

    lax.fori_loop(0, bb, one_request, 0)


def _gdn_sample(xp, gate, sm, conv_w, alog, dtb, onorm, s0, n_tok, bb):
    nb, n_heads = s0.shape[0], s0.shape[1]
    hw = n_heads * LANES
    qkv_w = 3 * hw
    return pl.pallas_call(
        functools.partial(_gdn_sample_kernel, n_heads, n_tok, bb),
        grid=(nb // bb,),
        in_specs=[pl.BlockSpec((bb, 2 * PAD_TOK, qkv_w), lambda i: (i, 0, 0)),
                  pl.BlockSpec((bb, PAD_TOK, hw), lambda i: (i, 0, 0)),
                  pl.BlockSpec((bb, PAD_TOK, 2 * LANES), lambda i: (i, 0, 0)),
                  pl.BlockSpec((CONV_K, qkv_w), lambda i: (0, 0)),
                  pl.BlockSpec((1, LANES), lambda i: (0, 0)),
                  pl.BlockSpec((1, LANES), lambda i: (0, 0)),
                  pl.BlockSpec((1, LANES), lambda i: (0, 0)),
                  pl.BlockSpec((bb, n_heads, LANES, LANES), lambda i: (i, 0, 0, 0))],
        out_specs=[pl.BlockSpec((bb, PAD_TOK, hw), lambda i: (i, 0, 0)),
                   pl.BlockSpec((bb, n_heads, LANES, LANES), lambda i: (i, 0, 0, 0))],
        out_shape=[jax.ShapeDtypeStruct((nb, PAD_TOK, hw), F32),
                   jax.ShapeDtypeStruct(s0.shape, F32)],
        compiler_params=_params(("parallel",)),
        name="gdn_sample",
    )(xp, gate, sm, conv_w, alog, dtb, onorm, s0)


def _head_rms(r, gain, seg_ones, head_dim):
    ss = _split_dot(r * r, seg_ones)
    return r * lax.rsqrt(ss * (1.0 / head_dim) + NORM_EPS) * gain


def _sb_qkv_kernel(head_dim, x_ref, g_ref, w_ref, qn_ref, kn_ref, seg_ref,
                   qb_ref, k_ref, v_ref, kb_ref, vb_ref, hn_ref):
    j = pl.program_id(1)

    @pl.when(j == 0)
    def _():
        hn_ref[...] = _rms_rows(x_ref[...], g_ref[...]).astype(BF16)

    r = jnp.dot(hn_ref[...], w_ref[...], preferred_element_type=F32)
    n_blk = r.shape[1] // LANES
    seg = seg_ref[...]

    @pl.when(j == 0)
    def _():
        for c in range(n_blk):
            cols = slice(c * LANES, (c + 1) * LANES)
            qn = _head_rms(r[:, cols], qn_ref[...], seg, head_dim)
            qb_ref[:, cols] = (qn * (head_dim ** -0.5)).astype(BF16)

    @pl.when(j == 1)
    def _():
        for c in range(n_blk):
            cols = slice(c * LANES, (c + 1) * LANES)
            kn = _head_rms(r[:, cols], kn_ref[...], seg, head_dim)
            k_ref[:, cols] = kn
            kb_ref[:, cols] = kn.astype(BF16)

    @pl.when(j == 2)
    def _():
        v_ref[...] = r
        vb_ref[...] = r.astype(BF16)


def _sb_qkv(x, gain, w, q_gain, k_gain, seg_ones, head_dim, tm):
    t, d = x.shape
    width = w.shape[1] // 3
    row_blk = pl.BlockSpec((tm, width), lambda i, j: (i, 0))
    return pl.pallas_call(
        functools.partial(_sb_qkv_kernel, head_dim),
        grid=(t // tm, 3),
        in_specs=[pl.BlockSpec((tm, d), lambda i, j: (i, 0)),
                  pl.BlockSpec((1, d), lambda i, j: (0, 0)),
                  pl.BlockSpec((d, width), lambda i, j: (0, j)),
                  pl.BlockSpec((1, LANES), lambda i, j: (0, 0)),
                  pl.BlockSpec((1, LANES), lambda i, j: (0, 0)),
                  pl.BlockSpec((LANES, LANES), lambda i, j: (0, 0))],
        out_specs=[row_blk, row_blk, row_blk, row_blk, row_blk],
        out_shape=[jax.ShapeDtypeStruct((t, width), BF16),
                   jax.ShapeDtypeStruct((t, width), F32),
                   jax.ShapeDtypeStruct((t, width), F32),
                   jax.ShapeDtypeStruct((t, width), BF16),
                   jax.ShapeDtypeStruct((t, width), BF16)],
        scratch_shapes=[pltpu.VMEM((tm, d), BF16)],
        compiler_params=_params(("parallel", "arbitrary")),
        name="sb_qkv",
    )(x, gain, w, q_gain, k_gain, seg_ones)


def _stick_block(z, carry, cum_ones, mask):
    tk = z.shape[1]
    sp = _softplus(z)
    lk = -sp if mask is None else jnp.where(mask, -sp, 0.0)
    cs = _split_dot(lk, cum_ones)
    a = jnp.exp((z - sp) + cs[:, :tk] + carry)
    if mask is not None:
        a = jnp.where(mask, a, 0.0)
    return a, carry + cs[:, tk:]


def _attn_prompt_kernel(head_dim, bias_ref, q_ref, k_ref, v_ref, suf_ref, o_ref,
                        acc_ref, carry_ref, z_ref, cs_ref, rs_ref):
    p = pl.program_id(1)
    qi = pl.program_id(2)
    tq = q_ref.shape[0]
    tk = suf_ref.shape[0]
    rows = 2 * tq
    q = q_ref[...].astype(F32)
    lane = lax.broadcasted_iota(jnp.int32, (tq, LANES), 1)
    first = lane < head_dim
    qq = jnp.concatenate([jnp.where(first, q, 0.0), jnp.where(first, 0.0, q)], axis=0).astype(BF16)
    row1 = lax.broadcasted_iota(jnp.int32, (rows, 1), 0)
    bias = jnp.where(row1 < tq, bias_ref[2 * p], bias_ref[2 * p + 1])
    acc_ref[...] = jnp.zeros(acc_ref.shape, F32)
    carry_ref[...] = jnp.zeros(carry_ref.shape, F32)

    def causal_mask():
        r2 = lax.broadcasted_iota(jnp.int32, (rows, tk), 0)
        c2 = lax.broadcasted_iota(jnp.int32, (rows, tk), 1)
        return c2 < jnp.where(r2 >= tq, r2 - tq, r2)

    def logits(j, zs):
        start = pl.multiple_of(j * tk, tk)
        kblk = k_ref[pl.ds(start, tk), :]
        z_ref[zs] = lax.dot_general(qq, kblk, (((1,), (1,)), ((), ())), preferred_element_type=F32) + bias

    def keeps(zs, cs_slot, masked):
        z = z_ref[zs]
        lk = jnp.minimum(-z, 0.0) - jnp.log(1.0 + jnp.exp2(jnp.abs(z) * (-LOG2E)))
        if masked:
            lk = jnp.where(causal_mask(), lk, 0.0)
        cs_ref[cs_slot] = jnp.dot(lk.astype(BF16), suf_ref[...], preferred_element_type=F32)
        rs_ref[cs_slot] = jnp.broadcast_to(jnp.sum(lk, axis=1, keepdims=True), (rows, LANES))

    def weigh(j, zs, cs_slot, masked, live=None):
        start = pl.multiple_of(j * tk, tk)
        vblk = v_ref[pl.ds(start, tk), :]
        carry = carry_ref[...]
        a = jnp.exp(z_ref[zs] + cs_ref[cs_slot] + jnp.concatenate([carry] * (tk // LANES), axis=1))
        if masked:
            a = jnp.where(causal_mask(), a, 0.0)
        rs = rs_ref[cs_slot]
        if live is not None:
            a = jnp.where(live, a, 0.0)
            rs = jnp.where(live, rs, 0.0)
        acc_ref[...] += jnp.dot(a.astype(BF16), vblk, preferred_element_type=F32)
        carry_ref[...] = carry + rs

    def next3(s):
        return jnp.where(s == 2, 0, s + 1)

    logits(qi, 2)
    logits(jnp.maximum(qi - 1, 0), 0)
    keeps(2, 1, True)
    logits(jnp.maximum(qi - 2, 0), 1)
    weigh(qi, 2, 1, True)
    keeps(0, 0, False)

    def trip(j, zc, cc):
        zb = next3(zc)
        weigh(j, zc, cc, False)
        keeps(zb, 1 - cc, False)
        logits(j - 2, next3(zb))
        return zb, 1 - cc

    def two_trips(t, slots):
        j = qi - 1 - 2 * t
        zc, cc = trip(j, *slots)
        return trip(j - 1, zc, cc)

    n_trips = jnp.maximum(qi - 2, 0)
    zc, cc = lax.fori_loop(0, lax.shift_right_logical(n_trips, 1), two_trips, (0, 0))
    slots = lax.cond((n_trips & 1) == 1, lambda s: trip(2, *s), lambda s: s, (zc, cc))
    zc, cc = slots

    zb = next3(zc)
    weigh(jnp.minimum(qi, 1), zc, cc, False, live=qi >= 2)
    keeps(zb, 1 - cc, False)
    weigh(0, zb, 1 - cc, False, live=qi >= 1)

    acc = acc_ref[...]
    o_ref[...] = jnp.where(first, acc[:tq], acc[tq:])


def _attn_prompt(qb, kb, vb, bias, suffix_ones, n_batch, seq, head_dim):
    width = qb.shape[1]
    tq = suffix_ones.shape[0]
    nq = seq // tq
    assert LANES == 2 * head_dim and seq % tq == 0
    return pl.pallas_call(
        functools.partial(_attn_prompt_kernel, head_dim),
        grid=(n_batch, width // LANES, nq),
        in_specs=[pl.BlockSpec(memory_space=pltpu.SMEM),
                  pl.BlockSpec((tq, LANES), lambda b, p, i: (b * nq + i, p)),
                  pl.BlockSpec((seq, LANES), lambda b, p, i: (b, p)),
                  pl.BlockSpec((seq, LANES), lambda b, p, i: (b, p)),
                  pl.BlockSpec(suffix_ones.shape, lambda b, p, i: (0, 0))],
        out_specs=pl.BlockSpec((tq, LANES), lambda b, p, i: (b * nq + i, p)),
        out_shape=jax.ShapeDtypeStruct((n_batch * seq, width), F32),
        scratch_shapes=[pltpu.VMEM((2 * tq, LANES), F32), pltpu.VMEM((2 * tq, LANES), F32),
                        pltpu.VMEM((3, 2 * tq, tq), F32), pltpu.VMEM((2, 2 * tq, tq), F32),
                        pltpu.VMEM((2, 2 * tq, LANES), F32)],
        compiler_params=_params(("parallel", "parallel", "arbitrary")),
        name="attn_prompt",
    )(bias, qb, kb, vb, suffix_ones)


def _attn_sample_kernel(head_dim, n_tok, n_kv, pt_ref, q_ref, kn_ref, vn_ref, *refs):
    kv_refs = refs[:n_kv]
    bias_ref, cum_ref, o_ref, qbd_ref, acc_ref, carry_ref = refs[n_kv:]
    j = pl.program_id(1)
    n_rows, width = qbd_ref.shape
    n_heads = width // head_dim
    row_head = lax.shift_right_logical(lax.broadcasted_iota(jnp.int32, (n_rows, width), 0),
                                       PAD_TOK.bit_length() - 1)
    col_head = lax.shift_right_logical(lax.broadcasted_iota(jnp.int32, (n_rows, width), 1),
                                       head_dim.bit_length() - 1)
    own = row_head == col_head
    bias = bias_ref[...]

    @pl.when(j == 0)
    def _():
        q8 = q_ref[0]
        qbd = jnp.where(own, jnp.concatenate([q8] * n_heads, axis=0), 0.0)
        qbd_ref[...] = qbd.astype(BF16)
        tok = lax.broadcasted_iota(jnp.int32, (n_rows, 1), 0) & (PAD_TOK - 1)
        bcol = bias[:, 0:1]
        lks, lss, valids = [], [], []
        for s in range(n_tok):
            z = jnp.sum(qbd * kn_ref[0, s:s + 1, :], axis=-1, keepdims=True) + bcol
            sp = _softplus(z)
            valid = tok > s
            valids.append(valid)
            lks.append(jnp.where(valid, -sp, 0.0))
            lss.append(z - sp)
        later = jnp.zeros((n_rows, 1), F32)
        acc = jnp.zeros((n_rows, width), F32)
        for s in reversed(range(n_tok)):
            a = jnp.where(valids[s], jnp.exp(lss[s] + later), 0.0)
            acc = acc + a * vn_ref[0, s:s + 1, :]
            later = later + lks[s]
        acc_ref[...] = acc
        carry_ref[...] = jnp.broadcast_to(later, carry_ref.shape)

    carry = carry_ref[...]
    acc = acc_ref[...]
    for k_ref, v_ref in zip(kv_refs[0::2], kv_refs[1::2]):
        z = jnp.dot(qbd_ref[...], k_ref[...].astype(BF16), preferred_element_type=F32) + bias
        a, carry = _stick_block(z, carry, cum_ref[...], None)
        acc = acc + lax.dot_general(a.astype(BF16), v_ref[...].astype(BF16), (((1,), (1,)), ((), ())),
                                    preferred_element_type=F32)
    carry_ref[...] = carry
    acc_ref[...] = acc

    @pl.when(j == pl.num_programs(1) - 1)
    def _():
        acc = jnp.where(own, acc_ref[...], 0.0)
        o = acc[0:PAD_TOK, :]
        for h in range(1, n_heads):
            o = o + acc[h * PAD_TOK:(h + 1) * PAD_TOK, :]
        o_ref[0] = o


def _attn_sample(page_table, q_pad, kn_pad, vn_pad, cache_k, cache_v, layer, bias_rows, cum_ones,
                 head_dim, n_tok):
    nb, n_pages = page_table.shape
    n_layers, n_pool, page, n_heads, _ = cache_k.shape
    width = n_heads * head_dim
    ck = jnp.transpose(cache_k, (0, 1, 3, 4, 2)).reshape(n_layers * n_pool, width, page)
    cv = jnp.transpose(cache_v, (0, 1, 3, 4, 2)).reshape(n_layers * n_pool, width, page)
    n_rows = n_heads * PAD_TOK
    base = layer * n_pool
    per_step = SAMPLE_PAGES_PER_STEP
    assert n_pages % per_step == 0

    def page_spec(g):
        def page_idx(b, j, pt):
            return (base + pt[b * n_pages + (n_pages - 1 - (j * per_step + g))], 0, 0)
        return pl.BlockSpec((None, width, page), page_idx)

    kv_specs, kv_args = [], []
    for g in range(per_step):
        kv_specs += [page_spec(g), page_spec(g)]
        kv_args += [ck, cv]
    tok_blk = pl.BlockSpec((1, PAD_TOK, width), lambda b, j, pt: (b, 0, 0))
    grid_spec = pltpu.PrefetchScalarGridSpec(
        num_scalar_prefetch=1,
        grid=(nb, n_pages // per_step),
        in_specs=[tok_blk, tok_blk, tok_blk] + kv_specs + [
            pl.BlockSpec((n_rows, LANES), lambda b, j, pt: (0, 0)),
            pl.BlockSpec(cum_ones.shape, lambda b, j, pt: (0, 0))],
        out_specs=tok_blk,
        scratch_shapes=[pltpu.VMEM((n_rows, width), BF16),
                        pltpu.VMEM((n_rows, width), F32),
                        pltpu.VMEM((n_rows, LANES), F32)],
    )
    return pl.pallas_call(
        functools.partial(_attn_sample_kernel, head_dim, n_tok, 2 * per_step),
        grid_spec=grid_spec,
        out_shape=jax.ShapeDtypeStruct((nb, PAD_TOK, width), F32),
        compiler_params=_params(("parallel", "arbitrary")),
        name="attn_sample",
    )(page_table.reshape(-1), q_pad, kn_pad, vn_pad, *kv_args, bias_rows, cum_ones)


def _pad_lanes(x, width):
    return jnp.pad(x, [(0, 0)] * (x.ndim - 1) + [(0, width - x.shape[-1])])


def _pad_tokens(x, total):
    return jnp.pad(x, [(0, 0), (0, total - x.shape[1]), (0, 0)])


def _cum_ones(tk):
    r = jnp.arange(tk)[:, None]
    c = jnp.arange(tk)[None, :]
    return jnp.concatenate([(r > c), jnp.ones((tk, tk), bool)], axis=1).astype(BF16)


def kernel(x_prompt, x_sample, cache_k, cache_v, state_delta, state_conv, page_table, norm_mix, norm_mlp,
           ab_w_in, ab_w_out, gmlp_w_s, gmlp_b_s, gmlp_v_norm, gdn_conv_w, gdn_a_log, gdn_dt_bias,
           gdn_out_norm, sb_w_qkv, sb_w_out, sb_q_norm, sb_k_norm, sb_logit_bias, mlp_w_up, mlp_w_down):
    n_batch, seq, d_model = x_prompt.shape
    nb_s, n_tok, _ = x_sample.shape
    depth = norm_mix.shape[0]
    n_groups = gmlp_w_s.shape[1]
    a_width = n_groups * LANES
    n_bheads = state_delta.shape[2]
    b_qkv = state_conv.shape[-1]
    b_width = n_bheads * LANES
    c_heads, head_dim = cache_k.shape[3], cache_k.shape[4]
    n_prompt = n_batch * seq
    n_sample = nb_s * n_tok
    assert a_width == b_width and b_qkv == 3 * b_width and LANES % head_dim == 0
    assert seq % ROW_TILE == 0 and n_sample % ROW_TILE == 0 and n_sample % GMLP_CHUNK == 0
    assert GMLP_CHUNK % n_tok == 0 and n_tok <= PAD_TOK

    y_p = x_prompt.reshape(n_prompt, d_model)
    y_s = x_sample.reshape(n_sample, d_model)
    assert cache_k.shape[2] == PAGE_BLOCK
    cum_ones = _cum_ones(PAGE_BLOCK)
    blk = jnp.arange(min(ATT_BLOCK, seq))
    suffix_ones = (blk[:, None] >= blk[None, :]).astype(BF16)
    seg = jnp.arange(LANES) // head_dim
    seg_ones = (seg[:, None] == seg[None, :]).astype(BF16)

    k_pr, v_pr, k_sa, v_sa = [], [], [], []
    d_pr, d_sa, c_pr, c_sa, gv_sa = [], [], [], [], []
    for layer in range(depth):
        li = layer // 2
        gain_mix = norm_mix[layer][None, :]
        if layer % 2 == 0:
            w_in = ab_w_in[li]
            o = 0
            a_u, o = w_in[:, o:o + a_width], o + a_width
            a_v, o = w_in[:, o:o + a_width], o + a_width
            qkv_w, o = w_in[:, o:o + b_qkv], o + b_qkv
            beta_w, o = w_in[:, o:o + n_bheads], o + n_bheads
            araw_w, o = w_in[:, o:o + n_bheads], o + n_bheads
            gate_w = w_in[:, o:]
            w_cat = jnp.concatenate([qkv_w, a_u, a_v, gate_w, _pad_lanes(beta_w, LANES),
                                     _pad_lanes(araw_w, LANES)], axis=1).astype(BF16)
            u_blk = b_qkv // a_width
            v_blk = u_blk + 1
            gate_blk = u_blk + 2
            sm_blk = (b_qkv + 3 * a_width) // (2 * LANES)
            tn = w_cat.shape[1] // 2
            w_out = ab_w_out[li].astype(BF16)
            alog = _pad_lanes(gdn_a_log[li][None, :], LANES)
            dtb = _pad_lanes(gdn_dt_bias[li][None, :], LANES)
            onorm = gdn_out_norm[li][None, :]
            conv_w = gdn_conv_w[li]
            v_norm = gmlp_v_norm[li][None, :]
            tril = jnp.tril(jnp.ones((GMLP_CHUNK, GMLP_CHUNK), bool))
            wmix_p = jnp.where(tril, gmlp_w_s[li], 0.0).astype(BF16)
            bias_p = jnp.broadcast_to(gmlp_b_s[li][:, :, None], (n_groups, GMLP_CHUNK, LANES))
            small = jnp.where(tril[:n_tok, :n_tok], gmlp_w_s[li][:, :n_tok, :n_tok], 0.0)
            eye_req = jnp.eye(GMLP_CHUNK // n_tok, dtype=F32)
            wmix_s = jnp.einsum('ab,gij->gaibj', eye_req, small).reshape(
                n_groups, GMLP_CHUNK, GMLP_CHUNK).astype(BF16)
            bias_s = jnp.broadcast_to(
                jnp.tile(gmlp_b_s[li][:, :n_tok], (1, GMLP_CHUNK // n_tok))[:, :, None],
                (n_groups, GMLP_CHUNK, LANES))

            proj_p = _norm_matmul(y_p, gain_mix, w_cat, ROW_TILE, tn)
            oa_p, _ = _gmlp(proj_p, u_blk, v_blk, v_norm, wmix_p, bias_p, ROW_TILE)
            ob_p, s_p = _gdn_prompt(proj_p, n_batch, seq, n_bheads, gate_blk, sm_blk, conv_w, alog, dtb, onorm)
            c_pr.append(proj_p[:, :b_qkv].reshape(n_batch, seq, b_qkv)[:, seq - (CONV_K - 1):])
            d_pr.append(s_p)
            y_p = _matmul_residual(y_p, [oa_p, ob_p], [w_out[:a_width], w_out[a_width:]], ROW_TILE)

            proj_s = _norm_matmul(y_s, gain_mix, w_cat, ROW_TILE, tn)
            oa_s, vn_s = _gmlp(proj_s, u_blk, v_blk, v_norm, wmix_s, bias_s, ROW_TILE)
            proj_s3 = proj_s.reshape(nb_s, n_tok, -1)
            qkv_s = proj_s3[:, :, :b_qkv]
            xp = _pad_tokens(jnp.concatenate([state_conv[li], qkv_s], axis=1), 2 * PAD_TOK)
            gate_s = _pad_tokens(proj_s3[:, :, gate_blk * a_width:(gate_blk + 1) * a_width], PAD_TOK)
            sm_s = _pad_tokens(proj_s3[:, :, sm_blk * 2 * LANES:], PAD_TOK)
            ob_s, s_s = _gdn_sample(xp, gate_s, sm_s, conv_w, alog, dtb, onorm, state_delta[li], n_tok, 8)
            ob_s = ob_s[:, :n_tok].reshape(n_sample, b_width)
            c_sa.append(qkv_s[:, n_tok - (CONV_K - 1):])
            d_sa.append(s_s)
            gv_sa.append(vn_s.reshape(nb_s, n_tok, a_width))
            y_s = _matmul_residual(y_s, [oa_s, ob_s], [w_out[:a_width], w_out[a_width:]], ROW_TILE)
        else:
            w_qkv = sb_w_qkv[li].astype(BF16)
            w_out = sb_w_out[li].astype(BF16)
            width = c_heads * head_dim
            q_gain = jnp.tile(sb_q_norm[li], LANES // head_dim)[None, :]
            k_gain = jnp.tile(sb_k_norm[li], LANES // head_dim)[None, :]
            bias = sb_logit_bias[li]

            qb, k_f, v_f, kb, vb = _sb_qkv(y_p, gain_mix, w_qkv, q_gain, k_gain, seg_ones, head_dim, ROW_TILE)
            o_p = _attn_prompt(qb, kb, vb, bias, suffix_ones, n_batch, seq, head_dim)
            k_pr.append(k_f.reshape(n_batch, seq, c_heads, head_dim))
            v_pr.append(v_f.reshape(n_batch, seq, c_heads, head_dim))
            y_p = _matmul_residual(y_p, [o_p], [w_out], ROW_TILE)

            qb, k_f, v_f, _, _ = _sb_qkv(y_s, gain_mix, w_qkv, q_gain, k_gain, seg_ones, head_dim, ROW_TILE)
            q_pad = _pad_tokens(qb.astype(F32).reshape(nb_s, n_tok, width), PAD_TOK)
            kn_pad = _pad_tokens(k_f.reshape(nb_s, n_tok, width), PAD_TOK)
            vn_pad = _pad_tokens(v_f.reshape(nb_s, n_tok, width), PAD_TOK)
            bias_rows = jnp.broadcast_to(jnp.repeat(bias, PAD_TOK)[:, None], (c_heads * PAD_TOK, LANES))
            o_s = _attn_sample(page_table, q_pad, kn_pad, vn_pad, cache_k, cache_v, li, bias_rows, cum_ones,
                               head_dim, n_tok)
            o_s = o_s[:, :n_tok].reshape(n_sample, width)
            k_sa.append(k_f.reshape(nb_s, n_tok, c_heads, head_dim))
            v_sa.append(v_f.reshape(nb_s, n_tok, c_heads, head_dim))
            y_s = _matmul_residual(y_s, [o_s], [w_out], ROW_TILE)

        gain_mlp = norm_mlp[layer][None, :]
        w_up = mlp_w_up[layer].astype(BF16)
        w_down = mlp_w_down[layer].astype(BF16)
        y_p = _mlp(y_p, gain_mlp, w_up, w_down, ROW_TILE, 1024)
        y_s = _mlp(y_s, gain_mlp, w_up, w_down, ROW_TILE, 1024)

    return (y_p.reshape(n_batch, seq, d_model), y_s.reshape(nb_s, n_tok, d_model),
            jnp.stack(k_pr), jnp.stack(v_pr), jnp.stack(k_sa), jnp.stack(v_sa),
            jnp.stack(d_pr), jnp.stack(d_sa), jnp.stack(c_pr), jnp.stack(c_sa), jnp.stack(gv_sa))
```

```python
import functools

import jax
import jax.numpy as jnp
from jax import lax
from jax.experimental import pallas as pl
from jax.experimental.pallas import tpu as pltpu

F32 = jnp.float32
BF16 = jnp.bfloat16
HI = lax.Precision.HIGHEST
NORM_EPS = 1e-6
LOG2E = 1.4426950408889634
LANES = 128
SUBLANES = 8
VMEM_LIMIT_BYTES = 48 * 1024 * 1024

GMLP_CHUNK = 128
GDN_CHUNK = 64
GDN_CHUNKS_PER_STEP = 2
CONV_K = 4
ROW_TILE = 512
ATT_BLOCK = 256
PAGE_BLOCK = 128
SAMPLE_PAGES_PER_STEP = 4
PAD_TOK = 8


def _params(semantics):
    return pltpu.CompilerParams(dimension_semantics=semantics, vmem_limit_bytes=VMEM_LIMIT_BYTES)


def _dot(a, b):
    return jnp.dot(a.astype(BF16), b.astype(BF16), preferred_element_type=F32)


def _dot_nt(a, b):
    return lax.dot_general(a.astype(BF16), b.astype(BF16), (((1,), (1,)), ((), ())),
                           preferred_element_type=F32)


def _dot_tn(a, b):
    return lax.dot_general(a.astype(BF16), b.astype(BF16), (((0,), (0,)), ((), ())),
                           preferred_element_type=F32)


def _dot32(a, b):
    return jnp.dot(a, b, precision=HI, preferred_element_type=F32)


def _dot32_nt(a, b):
    return lax.dot_general(a, b, (((1,), (1,)), ((), ())), precision=HI, preferred_element_type=F32)


def _split_dot(x, ones_mat):
    hi = x.astype(BF16)
    lo = (x - hi.astype(F32)).astype(BF16)
    return (jnp.dot(hi, ones_mat, preferred_element_type=F32)
            + jnp.dot(lo, ones_mat, preferred_element_type=F32))


def _sigmoid(x):
    return 1.0 / (1.0 + jnp.exp(-x))


def _silu(x):
    return x * _sigmoid(x)


def _softplus(x):
    return jnp.maximum(x, 0.0) + jnp.log1p(jnp.exp(-jnp.abs(x)))


def _gelu_tanh(x):
    return x * (0.5 * (1.0 + jnp.tanh(0.7978845608028654 * (x + 0.044715 * (x * x * x)))))


def _rms_rows(x, gain):
    ms = jnp.mean(x * x, axis=-1, keepdims=True)
    return x * lax.rsqrt(ms + NORM_EPS) * gain


def _norm_matmul_kernel(x_ref, g_ref, w_ref, o_ref, hn_ref):
    @pl.when(pl.program_id(1) == 0)
    def _():
        hn_ref[...] = _rms_rows(x_ref[...], g_ref[...]).astype(BF16)

    o_ref[...] = jnp.dot(hn_ref[...], w_ref[...], preferred_element_type=F32)


def _norm_matmul(x, gain, w, tm, tn):
    t, d = x.shape
    n = w.shape[1]
    return pl.pallas_call(
        _norm_matmul_kernel,
        grid=(t // tm, n // tn),
        in_specs=[pl.BlockSpec((tm, d), lambda i, j: (i, 0)),
                  pl.BlockSpec((1, d), lambda i, j: (0, 0)),
                  pl.BlockSpec((d, tn), lambda i, j: (0, j))],
        out_specs=pl.BlockSpec((tm, tn), lambda i, j: (i, j)),
        out_shape=jax.ShapeDtypeStruct((t, n), F32),
        scratch_shapes=[pltpu.VMEM((tm, d), BF16)],
        compiler_params=_params(("parallel", "arbitrary")),
        name="norm_matmul",
    )(x, gain, w)


def _matmul_residual_kernel(n_lhs, *refs):
    res_ref = refs[0]
    a_refs = refs[1:1 + n_lhs]
    w_refs = refs[1 + n_lhs:1 + 2 * n_lhs]
    o_ref = refs[1 + 2 * n_lhs]
    acc = res_ref[...]
    for a_ref, w_ref in zip(a_refs, w_refs):
        acc = acc + jnp.dot(a_ref[...].astype(BF16), w_ref[...], preferred_element_type=F32)
    o_ref[...] = acc


def _matmul_residual(res, lhs_list, w_list, tm):
    t, d = res.shape
    n_lhs = len(lhs_list)
    in_specs = [pl.BlockSpec((tm, d), lambda i: (i, 0))]
    in_specs += [pl.BlockSpec((tm, a.shape[1]), lambda i: (i, 0)) for a in lhs_list]
    in_specs += [pl.BlockSpec(w.shape, lambda i: (0, 0)) for w in w_list]
    return pl.pallas_call(
        functools.partial(_matmul_residual_kernel, n_lhs),
        grid=(t // tm,),
        in_specs=in_specs,
        out_specs=pl.BlockSpec((tm, d), lambda i: (i, 0)),
        out_shape=jax.ShapeDtypeStruct((t, d), F32),
        compiler_params=_params(("parallel",)),
        name="matmul_residual",
    )(res, *lhs_list, *w_list)


def _mlp_kernel(x_ref, g_ref, wu_ref, wd_ref, o_ref, hn_ref):
    f = pl.program_id(1)

    @pl.when(f == 0)
    def _():
        x = x_ref[...]
        hn_ref[...] = _rms_rows(x, g_ref[...]).astype(BF16)
        o_ref[...] = x

    h = jnp.dot(hn_ref[...], wu_ref[...], preferred_element_type=F32)
    h = jnp.maximum(h, 0.0)
    h = (h * h).astype(BF16)
    o_ref[...] += jnp.dot(h, wd_ref[...], preferred_element_type=F32)


def _mlp(x, gain, w_up, w_down, tm, tf):
    t, d = x.shape
    dff = w_up.shape[1]
    return pl.pallas_call(
        _mlp_kernel,
        grid=(t // tm, dff // tf),
        in_specs=[pl.BlockSpec((tm, d), lambda i, f: (i, 0)),
                  pl.BlockSpec((1, d), lambda i, f: (0, 0)),
                  pl.BlockSpec((d, tf), lambda i, f: (0, f)),
                  pl.BlockSpec((tf, d), lambda i, f: (f, 0))],
        out_specs=pl.BlockSpec((tm, d), lambda i, f: (i, 0)),
        out_shape=jax.ShapeDtypeStruct((t, d), F32),
        scratch_shapes=[pltpu.VMEM((tm, d), BF16)],
        compiler_params=_params(("parallel", "arbitrary")),
        name="mlp",
    )(x, gain, w_up, w_down)


def _gmlp_kernel(n_chunks, n_groups, u_ref, v_ref, vn_ref, wmix_ref, bias_ref, oa_ref, vout_ref):
    for c in range(n_chunks):
        rows = slice(c * GMLP_CHUNK, (c + 1) * GMLP_CHUNK)
        for g in range(n_groups):
            cols = slice(g * LANES, (g + 1) * LANES)
            vn = _rms_rows(_gelu_tanh(v_ref[rows, cols]), vn_ref[:, cols])
            vout_ref[rows, cols] = vn
            mixed = jnp.dot(wmix_ref[g], vn.astype(BF16), preferred_element_type=F32) + bias_ref[g]
            oa_ref[rows, cols] = _gelu_tanh(u_ref[rows, cols]) * mixed


def _gmlp(proj, u_blk, v_blk, v_norm, wmix, bias_full, rows_per_step):
    t = proj.shape[0]
    n_groups = wmix.shape[0]
    width = n_groups * LANES
    n_chunks = rows_per_step // GMLP_CHUNK
    return pl.pallas_call(
        functools.partial(_gmlp_kernel, n_chunks, n_groups),
        grid=(t // rows_per_step,),
        in_specs=[pl.BlockSpec((rows_per_step, width), lambda i: (i, u_blk)),
                  pl.BlockSpec((rows_per_step, width), lambda i: (i, v_blk)),
                  pl.BlockSpec((1, width), lambda i: (0, 0)),
                  pl.BlockSpec(wmix.shape, lambda i: (0, 0, 0)),
                  pl.BlockSpec(bias_full.shape, lambda i: (0, 0, 0))],
        out_specs=[pl.BlockSpec((rows_per_step, width), lambda i: (i, 0)),
                   pl.BlockSpec((rows_per_step, width), lambda i: (i, 0))],
        out_shape=[jax.ShapeDtypeStruct((t, width), F32), jax.ShapeDtypeStruct((t, width), F32)],
        compiler_params=_params(("parallel",)),
        name="gmlp",
    )(proj, proj, v_norm, wmix, bias_full)


def _l2norm_rows(x):
    return x * lax.rsqrt(jnp.sum(x * x, axis=-1, keepdims=True) + NORM_EPS)


def _gdn_gates(sm, alog, dtb):
    beta = _sigmoid(sm[:, :LANES])
    g = -jnp.exp(alog) * _softplus(sm[:, LANES:] + dtb)
    return beta, g


def _gdn_out(o, onorm, gate):
    return _rms_rows(o, onorm) * _silu(gate)


def _transpose_rows(x):
    rows = x.shape[0]
    if rows < LANES:
        x = jnp.concatenate([x, jnp.zeros((LANES - rows, LANES), F32)], axis=0)
    return jnp.transpose(x)


def _split3(x):
    hi = x.astype(BF16)
    return hi, (x - hi.astype(F32)).astype(BF16)


def _dot3(a, b, nt=False):
    dims = (((1,), (1,)), ((), ())) if nt else (((1,), (0,)), ((), ()))
    rows = a[0].shape[0]
    both = lax.dot_general(jnp.concatenate([a[0], a[1]], axis=0), b[0], dims, preferred_element_type=F32)
    return both[:rows] + both[rows:] + lax.dot_general(a[0], b[1], dims, preferred_element_type=F32)


def _block_diag(parts, n, mask):
    return tuple(jnp.concatenate([p] * n, axis=0) * mask for p in parts)


def _fold_diag_blocks(full, n, blk):
    r = lax.shift_right_logical(lax.broadcasted_iota(jnp.int32, full.shape, 0), blk.bit_length() - 1)
    c = lax.shift_right_logical(lax.broadcasted_iota(jnp.int32, full.shape, 1), blk.bit_length() - 1)
    kept = jnp.where(r == c, full, 0.0)
    out = kept[0:blk]
    for i in range(1, n):
        out = out + kept[i * blk:(i + 1) * blk]
    return out


def _inv_unit_lower_cat(l_cat, eye_cat, n, mask):
    size = l_cat.shape[0]
    x_bd = _block_diag(_split3(-l_cat), n, mask)
    x = -l_cat
    p = jnp.where(eye_cat, 1.0, 0.0) - l_cat
    m = 2
    while m < size:
        x = _dot3(_split3(x), x_bd)
        x_bd = _block_diag(_split3(x), n, mask)
        p = p + _dot3(_split3(p), x_bd)
        m *= 2
    return p


def _gdn_prompt_kernel(n_batch, n_heads, chunk, x_ref, gate_ref, sm_ref, cw_ref, alog_ref, dtb_ref, onorm_ref,
                       bd_ref, bd2_ref, o_ref, s_out_ref, xbuf_ref, s_ref):
    c_idx = pl.program_id(0)
    hw = n_heads * LANES
    cw = n_heads * chunk
    step_rows = x_ref.shape[1]
    n_chunks = step_rows // chunk

    @pl.when(c_idx == 0)
    def _():
        xbuf_ref[:, 0:SUBLANES, :] = jnp.zeros((n_batch, SUBLANES, xbuf_ref.shape[2]), F32)
        s_ref[...] = jnp.zeros(s_ref.shape, F32)

    r = lax.broadcasted_iota(jnp.int32, (chunk, chunk), 0)
    c = lax.broadcasted_iota(jnp.int32, (chunk, chunk), 1)
    incl_ones = jnp.where(c <= r, 1.0, 0.0)
    row_i = lax.broadcasted_iota(jnp.int32, (chunk, cw), 0)
    lane = lax.broadcasted_iota(jnp.int32, (chunk, cw), 1)
    lane_j = lane & (chunk - 1)
    lane_h = lax.shift_right_logical(lane, chunk.bit_length() - 1)
    incl_cat = lane_j <= row_i
    strict_cat = lane_j < row_i
    eye_cat = lane_j == row_i
    bd_mask = bd_ref[...]
    bd2_mask = bd2_ref[...]

    for b in range(n_batch):
        x = x_ref[b]
        xbuf_ref[b, SUBLANES:SUBLANES + step_rows, :] = x
        conv = cw_ref[CONV_K - 1:CONV_K, :] * x
        for i in range(CONV_K - 1):
            shift = CONV_K - 1 - i
            conv = conv + cw_ref[i:i + 1, :] * xbuf_ref[b, SUBLANES - shift:SUBLANES - shift + step_rows, :]
        xbuf_ref[b, 0:SUBLANES, :] = x[step_rows - SUBLANES:step_rows, :]
        act_all = _silu(conv)
        beta_all, g_all = _gdn_gates(sm_ref[b], alog_ref[...], dtb_ref[...])

        for cc in range(n_chunks):
            rows = slice(cc * chunk, (cc + 1) * chunk)
            act = act_all[rows]
            beta = beta_all[rows]
            g = g_all[rows]
            heads = range(n_heads)
            q = [_l2norm_rows(act[:, h * LANES:(h + 1) * LANES]) * (LANES ** -0.5) for h in heads]
            k = [_l2norm_rows(act[:, hw + h * LANES:hw + (h + 1) * LANES]) for h in heads]
            v = [act[:, 2 * hw + h * LANES:2 * hw + (h + 1) * LANES] for h in heads]
            b_col = [beta[:, h:h + 1] for h in heads]
            kb = [k[h] * b_col[h] for h in heads]

            g_cat = jnp.broadcast_to(g[:, 0:1], (chunk, cw))
            for h in range(1, n_heads):
                g_cat = jnp.where(lane_h == h, g[:, h:h + 1], g_cat)
            gcol_cat = _dot32(incl_ones, g_cat)
            grow_cat = jnp.sum(jnp.where(lane_j >= row_i, g_cat, 0.0), axis=0, keepdims=True)
            decay_cat = jnp.where(incl_cat, jnp.exp(jnp.where(incl_cat, gcol_cat - grow_cat, 0.0)), 0.0)
            gc_col = [gcol_cat[:, h * chunk:h * chunk + 1] for h in heads]
            egc = [jnp.exp(gc_col[h]) for h in heads]
            g_last = [gc_col[h][chunk - 1:chunk] for h in heads]

            k_stack = jnp.concatenate(k, axis=0)
            kk = _fold_diag_blocks(_dot3(_split3(jnp.concatenate(kb, axis=0)), _split3(k_stack), nt=True),
                                   n_heads, chunk)
            qk = _fold_diag_blocks(_dot_nt(jnp.concatenate(q, axis=0), k_stack), n_heads, chunk)
            t_cat = _inv_unit_lower_cat(jnp.where(strict_cat, kk * decay_cat, 0.0), eye_cat, n_heads, bd_mask)
            a_in = (qk * decay_cat).astype(BF16)

            rhs = jnp.concatenate([v[h] * b_col[h] for h in heads] + [kb[h] * egc[h] for h in heads], axis=1)
            uw = jnp.dot(t_cat.astype(BF16), jnp.concatenate([rhs.astype(BF16)] * n_heads, axis=0) * bd2_mask,
                         preferred_element_type=F32)

            s_old = [s_ref[b, h] for h in heads]
            v_new, o_inter = [], []
            for h in heads:
                wq = _dot(jnp.concatenate([uw[:, hw + h * LANES:hw + (h + 1) * LANES], q[h] * egc[h]], axis=0),
                          s_old[h])
                v_new.append(uw[:, h * LANES:(h + 1) * LANES] - wq[:chunk])
                o_inter.append(wq[chunk:])
            vn_cat = jnp.concatenate(v_new, axis=1).astype(BF16)
            o_intra = jnp.dot(a_in, jnp.concatenate([vn_cat] * n_heads, axis=0) * bd2_mask[:, :hw],
                              preferred_element_type=F32)
            for h in heads:
                cols = slice(h * LANES, (h + 1) * LANES)
                k_dec = k[h] * jnp.exp(g_last[h] - gc_col[h])
                v_pad = jnp.concatenate([v_new[h], jnp.zeros((LANES - chunk, LANES), F32)], axis=0)
                s_ref[b, h] = s_old[h] * jnp.exp(g_last[h]) + _dot(_transpose_rows(k_dec), v_pad)
                o_ref[b, rows, cols] = _gdn_out(o_inter[h] + o_intra[:, cols], onorm_ref[...],
                                                gate_ref[b, rows, cols])

    @pl.when(c_idx == pl.num_programs(0) - 1)
    def _():
        s_out_ref[...] = s_ref[...]


def _gdn_prompt(proj, n_batch, seq, n_heads, gate_blk, sm_blk, conv_w, alog, dtb, onorm):
    chunk = GDN_CHUNK
    step = min(GDN_CHUNKS_PER_STEP * chunk, seq)
    assert seq % step == 0 and step % chunk == 0
    qkv_w = 3 * n_heads * LANES
    hw = n_heads * LANES
    cw = n_heads * chunk
    proj3 = proj.reshape(n_batch, seq, proj.shape[1])
    blk_row = (jnp.arange(cw) // chunk)[:, None]
    bd_mask = (blk_row == (jnp.arange(cw) // chunk)[None, :]).astype(BF16)
    bd2_mask = (blk_row == ((jnp.arange(2 * hw) // LANES) % n_heads)[None, :]).astype(BF16)
    const = lambda a: pl.BlockSpec(a.shape, lambda c: (0,) * a.ndim)
    o, s_fin = pl.pallas_call(
        functools.partial(_gdn_prompt_kernel, n_batch, n_heads, chunk),
        grid=(seq // step,),
        in_specs=[pl.BlockSpec((n_batch, step, qkv_w), lambda c: (0, c, 0)),
                  pl.BlockSpec((n_batch, step, hw), lambda c: (0, c, gate_blk)),
                  pl.BlockSpec((n_batch, step, 2 * LANES), lambda c: (0, c, sm_blk)),
                  const(conv_w), const(alog), const(dtb), const(onorm), const(bd_mask), const(bd2_mask)],
        out_specs=[pl.BlockSpec((n_batch, step, hw), lambda c: (0, c, 0)),
                   pl.BlockSpec((n_batch, n_heads, LANES, LANES), lambda c: (0, 0, 0, 0))],
        out_shape=[jax.ShapeDtypeStruct((n_batch, seq, hw), F32),
                   jax.ShapeDtypeStruct((n_batch, n_heads, LANES, LANES), F32)],
        scratch_shapes=[pltpu.VMEM((n_batch, SUBLANES + step, qkv_w), F32),
                        pltpu.VMEM((n_batch, n_heads, LANES, LANES), F32)],
        compiler_params=_params(("arbitrary",)),
        name="gdn_prompt",
    )(proj3, proj3, proj3, conv_w, alog, dtb, onorm, bd_mask, bd2_mask)
    return o.reshape(n_batch * seq, hw), s_fin


def _gdn_sample_kernel(n_heads, n_tok, bb, xp_ref, gate_ref, sm_ref, cw_ref, alog_ref, dtb_ref, onorm_ref,
                       s0_ref, o_ref, s_out_ref):
    hw = n_heads * LANES

    def one_request(b, carry):
        conv = cw_ref[0:1, :] * xp_ref[b, 0:PAD_TOK, :]
        for i in range(1, CONV_K):
            conv = conv + cw_ref[i:i + 1, :] * xp_ref[b, i:i + PAD_TOK, :]
        act = _silu(conv)
        beta, g = _gdn_gates(sm_ref[b], alog_ref[...], dtb_ref[...])
        eg = jnp.exp(g)
        gate = gate_ref[b]
        ks = [_l2norm_rows(act[:, hw + h * LANES:hw + (h + 1) * LANES]) for h in range(n_heads)]
        qs = [_l2norm_rows(act[:, h * LANES:(h + 1) * LANES]) * (LANES ** -0.5) for h in range(n_heads)]
        kq_t = _transpose_rows(jnp.concatenate(ks + qs, axis=0))
        for h in range(n_heads):
            v_h = act[:, 2 * hw + h * LANES:2 * hw + (h + 1) * LANES]
            s = s0_ref[b, h]
            outs = []
            for t in range(n_tok):
                kc = kq_t[:, h * PAD_TOK + t:h * PAD_TOK + t + 1]
                qc = kq_t[:, (n_heads + h) * PAD_TOK + t:(n_heads + h) * PAD_TOK + t + 1]
                e = eg[t:t + 1, h:h + 1]
                ks = jnp.sum(s * kc, axis=0, keepdims=True)
                d = beta[t:t + 1, h:h + 1] * (v_h[t:t + 1, :] - e * ks)
                s = e * s + kc * d
                outs.append(jnp.sum(s * qc, axis=0, keepdims=True))
            outs.append(jnp.zeros((PAD_TOK - n_tok, LANES), F32))
            o = jnp.concatenate(outs, axis=0)
            s_out_ref[b, h] = s
            o_ref[b, :, h * LANES:(h + 1) * LANES] = _gdn_out(
                o, onorm_ref[...], gate[:, h * LANES:(h + 1) * LANES])
        return carry

    lax.fori_loop(0, bb, one_request, 0)


def _gdn_sample(xp, gate, sm, conv_w, alog, dtb, onorm, s0, n_tok, bb):
    nb, n_heads = s0.shape[0], s0.shape[1]
    hw = n_heads * LANES
    qkv_w = 3 * hw
    return pl.pallas_call(
        functools.partial(_gdn_sample_kernel, n_heads, n_tok, bb),
        grid=(nb // bb,),
        in_specs=[pl.BlockSpec((bb, 2 * PAD_TOK, qkv_w), lambda i: (i, 0, 0)),
                  pl.BlockSpec((bb, PAD_TOK, hw), lambda i: (i, 0, 0)),
                  pl.BlockSpec((bb, PAD_TOK, 2 * LANES), lambda i: (i, 0, 0)),
                  pl.BlockSpec((CONV_K, qkv_w), lambda i: (0, 0)),
                  pl.BlockSpec((1, LANES), lambda i: (0, 0)),
                  pl.BlockSpec((1, LANES), lambda i: (0, 0)),
                  pl.BlockSpec((1, LANES), lambda i: (0, 0)),
                  pl.BlockSpec((bb, n_heads, LANES, LANES), lambda i: (i, 0, 0, 0))],
        out_specs=[pl.BlockSpec((bb, PAD_TOK, hw), lambda i: (i, 0, 0)),
                   pl.BlockSpec((bb, n_heads, LANES, LANES), lambda i: (i, 0, 0, 0))],
        out_shape=[jax.ShapeDtypeStruct((nb, PAD_TOK, hw), F32),
                   jax.ShapeDtypeStruct(s0.shape, F32)],
        compiler_params=_params(("parallel",)),
        name="gdn_sample",
    )(xp, gate, sm, conv_w, alog, dtb, onorm, s0)


def _head_rms(r, gain, seg_ones, head_dim):
    ss = _split_dot(r * r, seg_ones)
    return r * lax.rsqrt(ss * (1.0 / head_dim) + NORM_EPS) * gain


def _sb_qkv_kernel(head_dim, x_ref, g_ref, w_ref, qn_ref, kn_ref, seg_ref,
                   qb_ref, k_ref, v_ref, kb_ref, vb_ref, hn_ref):
    j = pl.program_id(1)

    @pl.when(j == 0)
    def _():
        hn_ref[...] = _rms_rows(x_ref[...], g_ref[...]).astype(BF16)

    r = jnp.dot(hn_ref[...], w_ref[...], preferred_element_type=F32)
    n_blk = r.shape[1] // LANES
    seg = seg_ref[...]

    @pl.when(j == 0)
    def _():
        for c in range(n_blk):
            cols = slice(c * LANES, (c + 1) * LANES)
            qn = _head_rms(r[:, cols], qn_ref[...], seg, head_dim)
            qb_ref[:, cols] = (qn * (head_dim ** -0.5)).astype(BF16)

    @pl.when(j == 1)
    def _():
        for c in range(n_blk):
            cols = slice(c * LANES, (c + 1) * LANES)
            kn = _head_rms(r[:, cols], kn_ref[...], seg, head_dim)
            k_ref[:, cols] = kn
            kb_ref[:, cols] = kn.astype(BF16)

    @pl.when(j == 2)
    def _():
        v_ref[...] = r
        vb_ref[...] = r.astype(BF16)


def _sb_qkv(x, gain, w, q_gain, k_gain, seg_ones, head_dim, tm):
    t, d = x.shape
    width = w.shape[1] // 3
    row_blk = pl.BlockSpec((tm, width), lambda i, j: (i, 0))
    return pl.pallas_call(
        functools.partial(_sb_qkv_kernel, head_dim),
        grid=(t // tm, 3),
        in_specs=[pl.BlockSpec((tm, d), lambda i, j: (i, 0)),
                  pl.BlockSpec((1, d), lambda i, j: (0, 0)),
                  pl.BlockSpec((d, width), lambda i, j: (0, j)),
                  pl.BlockSpec((1, LANES), lambda i, j: (0, 0)),
                  pl.BlockSpec((1, LANES), lambda i, j: (0, 0)),
                  pl.BlockSpec((LANES, LANES), lambda i, j: (0, 0))],
        out_specs=[row_blk, row_blk, row_blk, row_blk, row_blk],
        out_shape=[jax.ShapeDtypeStruct((t, width), BF16),
                   jax.ShapeDtypeStruct((t, width), F32),
                   jax.ShapeDtypeStruct((t, width), F32),
                   jax.ShapeDtypeStruct((t, width), BF16),
                   jax.ShapeDtypeStruct((t, width), BF16)],
        scratch_shapes=[pltpu.VMEM((tm, d), BF16)],
        compiler_params=_params(("parallel", "arbitrary")),
        name="sb_qkv",
    )(x, gain, w, q_gain, k_gain, seg_ones)


def _attn_prompt_kernel(head_dim, bias_ref, q_ref, k_ref, v_ref, suf_ref, o_ref,
                        acc_ref, carry_ref, z_ref, cs_ref, rs_ref):
    p = pl.program_id(1)
    qi = pl.program_id(2)
    tq = q_ref.shape[0]
    tk = suf_ref.shape[0]
    rows = 2 * tq
    q = q_ref[...].astype(F32)
    lane = lax.broadcasted_iota(jnp.int32, (tq, LANES), 1)
    first = lane < head_dim
    qq = jnp.concatenate([jnp.where(first, q, 0.0), jnp.where(first, 0.0, q)], axis=0).astype(BF16)
    row1 = lax.broadcasted_iota(jnp.int32, (rows, 1), 0)
    bias = jnp.where(row1 < tq, bias_ref[2 * p], bias_ref[2 * p + 1])
    acc_ref[...] = jnp.zeros(acc_ref.shape, F32)
    carry_ref[...] = jnp.zeros(carry_ref.shape, F32)

    def causal_mask():
        r2 = lax.broadcasted_iota(jnp.int32, (rows, tk), 0)
        c2 = lax.broadcasted_iota(jnp.int32, (rows, tk), 1)
        return c2 < jnp.where(r2 >= tq, r2 - tq, r2)

    def logits(j, zs):
        start = pl.multiple_of(j * tk, tk)
        kblk = k_ref[pl.ds(start, tk), :]
        z_ref[zs] = lax.dot_general(qq, kblk, (((1,), (1,)), ((), ())), preferred_element_type=F32) + bias

    def keeps(zs, cs_slot, masked):
        z = z_ref[zs]
        lk = jnp.minimum(-z, 0.0) - jnp.log(1.0 + jnp.exp2(jnp.abs(z) * (-LOG2E)))
        if masked:
            lk = jnp.where(causal_mask(), lk, 0.0)
        cs_ref[cs_slot] = jnp.dot(lk.astype(BF16), suf_ref[...], preferred_element_type=F32)
        rs_ref[cs_slot] = jnp.broadcast_to(jnp.sum(lk, axis=1, keepdims=True), (rows, LANES))

    def weigh(j, zs, cs_slot, masked, live=None):
        start = pl.multiple_of(j * tk, tk)
        vblk = v_ref[pl.ds(start, tk), :]
        carry = carry_ref[...]
        a = jnp.exp(z_ref[zs] + cs_ref[cs_slot] + jnp.concatenate([carry] * (tk // LANES), axis=1))
        if masked:
            a = jnp.where(causal_mask(), a, 0.0)
        rs = rs_ref[cs_slot]
        if live is not None:
            a = jnp.where(live, a, 0.0)
            rs = jnp.where(live, rs, 0.0)
        acc_ref[...] += jnp.dot(a.astype(BF16), vblk, preferred_element_type=F32)
        carry_ref[...] = carry + rs

    def next3(s):
        return jnp.where(s == 2, 0, s + 1)

    logits(qi, 2)
    logits(jnp.maximum(qi - 1, 0), 0)
    keeps(2, 1, True)
    logits(jnp.maximum(qi - 2, 0), 1)
    weigh(qi, 2, 1, True)
    keeps(0, 0, False)

    def trip(j, zc, cc):
        zb = next3(zc)
        weigh(j, zc, cc, False)
        keeps(zb, 1 - cc, False)
        logits(j - 2, next3(zb))
        return zb, 1 - cc

    def two_trips(t, slots):
        j = qi - 1 - 2 * t
        zc, cc = trip(j, *slots)
        return trip(j - 1, zc, cc)

    n_trips = jnp.maximum(qi - 2, 0)
    zc, cc = lax.fori_loop(0, lax.shift_right_logical(n_trips, 1), two_trips, (0, 0))
    slots = lax.cond((n_trips & 1) == 1, lambda s: trip(2, *s), lambda s: s, (zc, cc))
    zc, cc = slots

    zb = next3(zc)
    weigh(jnp.minimum(qi, 1), zc, cc, False, live=qi >= 2)
    keeps(zb, 1 - cc, False)
    weigh(0, zb, 1 - cc, False, live=qi >= 1)

    acc = acc_ref[...]
    o_ref[...] = jnp.where(first, acc[:tq], acc[tq:])


def _attn_prompt(qb, kb, vb, bias, suffix_ones, n_batch, seq, head_dim):
    width = qb.shape[1]
    tq = suffix_ones.shape[0]
    nq = seq // tq
    assert LANES == 2 * head_dim and seq % tq == 0
    return pl.pallas_call(
        functools.partial(_attn_prompt_kernel, head_dim),
        grid=(n_batch, width // LANES, nq),
        in_specs=[pl.BlockSpec(memory_space=pltpu.SMEM),
                  pl.BlockSpec((tq, LANES), lambda b, p, i: (b * nq + i, p)),
                  pl.BlockSpec((seq, LANES), lambda b, p, i: (b, p)),
                  pl.BlockSpec((seq, LANES), lambda b, p, i: (b, p)),
                  pl.BlockSpec(suffix_ones.shape, lambda b, p, i: (0, 0))],
        out_specs=pl.BlockSpec((tq, LANES), lambda b, p, i: (b * nq + i, p)),
        out_shape=jax.ShapeDtypeStruct((n_batch * seq, width), F32),
        scratch_shapes=[pltpu.VMEM((2 * tq, LANES), F32), pltpu.VMEM((2 * tq, LANES), F32),
                        pltpu.VMEM((3, 2 * tq, tq), F32), pltpu.VMEM((2, 2 * tq, tq), F32),
                        pltpu.VMEM((2, 2 * tq, LANES), F32)],
        compiler_params=_params(("parallel", "parallel", "arbitrary")),
        name="attn_prompt",
    )(bias, qb, kb, vb, suffix_ones)


def _attn_sample_kernel(head_dim, n_tok, n_kv, pt_ref, qbd_ref, kn_ref, vn_ref, *refs):
    kv_refs = refs[:n_kv]
    bias_ref, suf_ref, o_ref, acc_ref, carry_ref = refs[n_kv:]
    j = pl.program_id(1)
    n_rows, width = acc_ref.shape
    row_head = lax.shift_right_logical(lax.broadcasted_iota(jnp.int32, (n_rows, width), 0),
                                       n_tok.bit_length() - 1)
    col_head = lax.shift_right_logical(lax.broadcasted_iota(jnp.int32, (n_rows, width), 1),
                                       head_dim.bit_length() - 1)
    own = row_head == col_head
    bias = bias_ref[...]
    qbd = qbd_ref[0]

    @pl.when(j == 0)
    def _():
        qf = qbd.astype(F32)
        tok = lax.broadcasted_iota(jnp.int32, (n_rows, 1), 0) & (n_tok - 1)
        bcol = bias[:, 0:1]
        lks, lss, valids = [], [], []
        for s in range(n_tok):
            z = jnp.sum(qf * kn_ref[0, s:s + 1, :], axis=-1, keepdims=True) + bcol
            sp = _softplus(z)
            valid = tok > s
            valids.append(valid)
            lks.append(jnp.where(valid, -sp, 0.0))
            lss.append(z - sp)
        later = jnp.zeros((n_rows, 1), F32)
        acc = jnp.zeros((n_rows, width), F32)
        for s in reversed(range(n_tok)):
            a = jnp.where(valids[s], jnp.exp(lss[s] + later), 0.0)
            acc = acc + a * vn_ref[0, s:s + 1, :]
            later = later + lks[s]
        acc_ref[...] = acc
        carry_ref[...] = jnp.broadcast_to(later, carry_ref.shape)

    k_refs, v_refs = kv_refs[0::2], kv_refs[1::2]
    zs = [jnp.dot(qbd, k_ref[...].astype(BF16), preferred_element_type=F32) + bias for k_ref in k_refs]
    lks = [jnp.minimum(-z, 0.0) - jnp.log(1.0 + jnp.exp2(jnp.abs(z) * (-LOG2E))) for z in zs]
    css = [jnp.dot(lk.astype(BF16), suf_ref[...], preferred_element_type=F32) for lk in lks]
    carry = carry_ref[...]
    acc = acc_ref[...]
    for z, lk, cs, v_ref in zip(zs, lks, css, v_refs):
        a = jnp.exp(z + cs + carry)
        acc = acc + lax.dot_general(a.astype(BF16), v_ref[...].astype(BF16), (((1,), (1,)), ((), ())),
                                    preferred_element_type=F32)
        carry = carry + jnp.sum(lk, axis=1, keepdims=True)
    carry_ref[...] = carry
    acc_ref[...] = acc

    @pl.when(j == pl.num_programs(1) - 1)
    def _():
        own_acc = jnp.where(own, acc_ref[...], 0.0)
        sel_r = lax.broadcasted_iota(jnp.int32, (SUBLANES, n_rows), 0)
        sel_c = lax.broadcasted_iota(jnp.int32, (SUBLANES, n_rows), 1) & (n_tok - 1)
        o = _dot32(jnp.where(sel_r == sel_c, 1.0, 0.0), own_acc)
        o_ref[0] = o[:n_tok]


def _attn_sample(page_table, qbd, k_new, v_new, cache_k, cache_v, layer, bias_rows, suffix_ones,
                 head_dim, n_tok):
    nb, n_pages = page_table.shape
    n_layers, n_pool, page, n_heads, _ = cache_k.shape
    width = n_heads * head_dim
    ck = jnp.transpose(cache_k, (0, 1, 3, 4, 2)).reshape(n_layers * n_pool, width, page)
    cv = jnp.transpose(cache_v, (0, 1, 3, 4, 2)).reshape(n_layers * n_pool, width, page)
    n_rows = n_heads * n_tok
    base = layer * n_pool
    per_step = min(SAMPLE_PAGES_PER_STEP, n_pages)
    assert n_pages % per_step == 0 and n_tok & (n_tok - 1) == 0 and n_tok <= SUBLANES

    def page_spec(g):
        def page_idx(b, j, pt):
            return (base + pt[b * n_pages + (n_pages - 1 - (j * per_step + g))], 0, 0)
        return pl.BlockSpec((None, width, page), page_idx)

    kv_specs, kv_args = [], []
    for g in range(per_step):
        kv_specs += [page_spec(g), page_spec(g)]
        kv_args += [ck, cv]
    tok_blk = pl.BlockSpec((1, n_tok, width), lambda b, j, pt: (b, 0, 0))
    grid_spec = pltpu.PrefetchScalarGridSpec(
        num_scalar_prefetch=1,
        grid=(nb, n_pages // per_step),
        in_specs=[pl.BlockSpec((1, n_rows, width), lambda b, j, pt: (b, 0, 0)), tok_blk, tok_blk] + kv_specs + [
            pl.BlockSpec((n_rows, LANES), lambda b, j, pt: (0, 0)),
            pl.BlockSpec(suffix_ones.shape, lambda b, j, pt: (0, 0))],
        out_specs=tok_blk,
        scratch_shapes=[pltpu.VMEM((n_rows, width), F32),
                        pltpu.VMEM((n_rows, LANES), F32)],
    )
    return pl.pallas_call(
        functools.partial(_attn_sample_kernel, head_dim, n_tok, 2 * per_step),
        grid_spec=grid_spec,
        out_shape=jax.ShapeDtypeStruct((nb, n_tok, width), F32),
        compiler_params=_params(("parallel", "arbitrary")),
        name="attn_sample",
    )(page_table.reshape(-1), qbd, k_new, v_new, *kv_args, bias_rows, suffix_ones)


def _pad_lanes(x, width):
    return jnp.pad(x, [(0, 0)] * (x.ndim - 1) + [(0, width - x.shape[-1])])


def _pad_tokens(x, total):
    return jnp.pad(x, [(0, 0), (0, total - x.shape[1]), (0, 0)])


def kernel(x_prompt, x_sample, cache_k, cache_v, state_delta, state_conv, page_table, norm_mix, norm_mlp,
           ab_w_in, ab_w_out, gmlp_w_s, gmlp_b_s, gmlp_v_norm, gdn_conv_w, gdn_a_log, gdn_dt_bias,
           gdn_out_norm, sb_w_qkv, sb_w_out, sb_q_norm, sb_k_norm, sb_logit_bias, mlp_w_up, mlp_w_down):
    n_batch, seq, d_model = x_prompt.shape
    nb_s, n_tok, _ = x_sample.shape
    depth = norm_mix.shape[0]
    n_groups = gmlp_w_s.shape[1]
    a_width = n_groups * LANES
    n_bheads = state_delta.shape[2]
    b_qkv = state_conv.shape[-1]
    b_width = n_bheads * LANES
    c_heads, head_dim = cache_k.shape[3], cache_k.shape[4]
    n_prompt = n_batch * seq
    n_sample = nb_s * n_tok
    assert a_width == b_width and b_qkv == 3 * b_width and LANES % head_dim == 0
    assert seq % ROW_TILE == 0 and n_sample % ROW_TILE == 0 and n_sample % GMLP_CHUNK == 0
    assert GMLP_CHUNK % n_tok == 0 and n_tok <= PAD_TOK

    y_p = x_prompt.reshape(n_prompt, d_model)
    y_s = x_sample.reshape(n_sample, d_model)
    assert cache_k.shape[2] == PAGE_BLOCK
    pos = jnp.arange(PAGE_BLOCK)
    page_suffix_ones = (pos[:, None] >= pos[None, :]).astype(BF16)
    blk = jnp.arange(min(ATT_BLOCK, seq))
    suffix_ones = (blk[:, None] >= blk[None, :]).astype(BF16)
    seg = jnp.arange(LANES) // head_dim
    seg_ones = (seg[:, None] == seg[None, :]).astype(BF16)

    k_pr, v_pr, k_sa, v_sa = [], [], [], []
    d_pr, d_sa, c_pr, c_sa, gv_sa = [], [], [], [], []
    for layer in range(depth):
        li = layer // 2
        gain_mix = norm_mix[layer][None, :]
        if layer % 2 == 0:
            w_in = ab_w_in[li]
            o = 0
            a_u, o = w_in[:, o:o + a_width], o + a_width
            a_v, o = w_in[:, o:o + a_width], o + a_width
            qkv_w, o = w_in[:, o:o + b_qkv], o + b_qkv
            beta_w, o = w_in[:, o:o + n_bheads], o + n_bheads
            araw_w, o = w_in[:, o:o + n_bheads], o + n_bheads
            gate_w = w_in[:, o:]
            w_cat = jnp.concatenate([qkv_w, a_u, a_v, gate_w, _pad_lanes(beta_w, LANES),
                                     _pad_lanes(araw_w, LANES)], axis=1).astype(BF16)
            u_blk = b_qkv // a_width
            v_blk = u_blk + 1
            gate_blk = u_blk + 2
            sm_blk = (b_qkv + 3 * a_width) // (2 * LANES)
            tn = w_cat.shape[1] // 2
            w_out = ab_w_out[li].astype(BF16)
            alog = _pad_lanes(gdn_a_log[li][None, :], LANES)
            dtb = _pad_lanes(gdn_dt_bias[li][None, :], LANES)
            onorm = gdn_out_norm[li][None, :]
            conv_w = gdn_conv_w[li]
            v_norm = gmlp_v_norm[li][None, :]
            tril = jnp.tril(jnp.ones((GMLP_CHUNK, GMLP_CHUNK), bool))
            wmix_p = jnp.where(tril, gmlp_w_s[li], 0.0).astype(BF16)
            bias_p = jnp.broadcast_to(gmlp_b_s[li][:, :, None], (n_groups, GMLP_CHUNK, LANES))
            small = jnp.where(tril[:n_tok, :n_tok], gmlp_w_s[li][:, :n_tok, :n_tok], 0.0)
            eye_req = jnp.eye(GMLP_CHUNK // n_tok, dtype=F32)
            wmix_s = jnp.einsum('ab,gij->gaibj', eye_req, small).reshape(
                n_groups, GMLP_CHUNK, GMLP_CHUNK).astype(BF16)
            bias_s = jnp.broadcast_to(
                jnp.tile(gmlp_b_s[li][:, :n_tok], (1, GMLP_CHUNK // n_tok))[:, :, None],
                (n_groups, GMLP_CHUNK, LANES))

            proj_p = _norm_matmul(y_p, gain_mix, w_cat, ROW_TILE, tn)
            oa_p, _ = _gmlp(proj_p, u_blk, v_blk, v_norm, wmix_p, bias_p, ROW_TILE)
            ob_p, s_p = _gdn_prompt(proj_p, n_batch, seq, n_bheads, gate_blk, sm_blk, conv_w, alog, dtb, onorm)
            c_pr.append(proj_p[:, :b_qkv].reshape(n_batch, seq, b_qkv)[:, seq - (CONV_K - 1):])
            d_pr.append(s_p)
            y_p = _matmul_residual(y_p, [oa_p, ob_p], [w_out[:a_width], w_out[a_width:]], ROW_TILE)

            proj_s = _norm_matmul(y_s, gain_mix, w_cat, ROW_TILE, tn)
            oa_s, vn_s = _gmlp(proj_s, u_blk, v_blk, v_norm, wmix_s, bias_s, ROW_TILE)
            proj_s3 = proj_s.reshape(nb_s, n_tok, -1)
            qkv_s = proj_s3[:, :, :b_qkv]
            xp = _pad_tokens(jnp.concatenate([state_conv[li], qkv_s], axis=1), 2 * PAD_TOK)
            gate_s = _pad_tokens(proj_s3[:, :, gate_blk * a_width:(gate_blk + 1) * a_width], PAD_TOK)
            sm_s = _pad_tokens(proj_s3[:, :, sm_blk * 2 * LANES:], PAD_TOK)
            ob_s, s_s = _gdn_sample(xp, gate_s, sm_s, conv_w, alog, dtb, onorm, state_delta[li], n_tok, 8)
            ob_s = ob_s[:, :n_tok].reshape(n_sample, b_width)
            c_sa.append(qkv_s[:, n_tok - (CONV_K - 1):])
            d_sa.append(s_s)
            gv_sa.append(vn_s.reshape(nb_s, n_tok, a_width))
            y_s = _matmul_residual(y_s, [oa_s, ob_s], [w_out[:a_width], w_out[a_width:]], ROW_TILE)
        else:
            w_qkv = sb_w_qkv[li].astype(BF16)
            w_out = sb_w_out[li].astype(BF16)
            width = c_heads * head_dim
            q_gain = jnp.tile(sb_q_norm[li], LANES // head_dim)[None, :]
            k_gain = jnp.tile(sb_k_norm[li], LANES // head_dim)[None, :]
            bias = sb_logit_bias[li]

            qb, k_f, v_f, kb, vb = _sb_qkv(y_p, gain_mix, w_qkv, q_gain, k_gain, seg_ones, head_dim, ROW_TILE)
            o_p = _attn_prompt(qb, kb, vb, bias, suffix_ones, n_batch, seq, head_dim)
            k_pr.append(k_f.reshape(n_batch, seq, c_heads, head_dim))
            v_pr.append(v_f.reshape(n_batch, seq, c_heads, head_dim))
            y_p = _matmul_residual(y_p, [o_p], [w_out], ROW_TILE)

            qb, k_f, v_f, _, _ = _sb_qkv(y_s, gain_mix, w_qkv, q_gain, k_gain, seg_ones, head_dim, ROW_TILE)
            head_cols = (jnp.arange(width) // head_dim)[None, :] == jnp.arange(c_heads)[:, None]
            q_s = qb.reshape(nb_s, 1, n_tok, width)
            qbd = jnp.where(head_cols[None, :, None, :], q_s, jnp.zeros_like(q_s)).reshape(
                nb_s, c_heads * n_tok, width)
            bias_rows = jnp.broadcast_to(jnp.repeat(bias, n_tok)[:, None], (c_heads * n_tok, LANES))
            o_s = _attn_sample(page_table, qbd, k_f.reshape(nb_s, n_tok, width), v_f.reshape(nb_s, n_tok, width),
                               cache_k, cache_v, li, bias_rows, page_suffix_ones, head_dim, n_tok)
            o_s = o_s.reshape(n_sample, width)
            k_sa.append(k_f.reshape(nb_s, n_tok, c_heads, head_dim))
            v_sa.append(v_f.reshape(nb_s, n_tok, c_heads, head_dim))
            y_s = _matmul_residual(y_s, [o_s], [w_out], ROW_TILE)

        gain_mlp = norm_mlp[layer][None, :]
        w_up = mlp_w_up[layer].astype(BF16)
        w_down = mlp_w_down[layer].astype(BF16)
        y_p = _mlp(y_p, gain_mlp, w_up, w_down, ROW_TILE, 1024)
        y_s = _mlp(y_s, gain_mlp, w_up, w_down, ROW_TILE, 1024)

    return (y_p.reshape(n_batch, seq, d_model), y_s.reshape(nb_s, n_tok, d_model),
            jnp.stack(k_pr), jnp.stack(v_pr), jnp.stack(k_sa), jnp.stack(v_sa),
            jnp.stack(d_pr), jnp.stack(d_sa), jnp.stack(c_pr), jnp.stack(c_sa), jnp.stack(gv_sa))
```

```python
import functools

import jax
import jax.numpy as jnp
from jax import lax
from jax.experimental import pallas as pl
from jax.experimental.pallas import tpu as pltpu

F32 = jnp.float32
BF16 = jnp.bfloat16
HI = lax.Precision.HIGHEST
NORM_EPS = 1e-6
LOG2E = 1.4426950408889634
LANES = 128
SUBLANES = 8
VMEM_LIMIT_BYTES = 48 * 1024 * 1024

GMLP_CHUNK = 128
GDN_CHUNK = 64
GDN_CHUNKS_PER_STEP = 2
CONV_K = 4
ROW_TILE = 512
ATT_BLOCK = 256
ATT_PAIRS_PER_STEP = 1
ATT_TRIPS_PER_BODY = 2
PAGE_BLOCK = 128
SAMPLE_PAGES_PER_STEP = 8
PAD_TOK = 8


def _params(semantics):
    return pltpu.CompilerParams(dimension_semantics=semantics, vmem_limit_bytes=VMEM_LIMIT_BYTES)


def _dot(a, b):
    return jnp.dot(a.astype(BF16), b.astype(BF16), preferred_element_type=F32)


def _dot_nt(a, b):
    return lax.dot_general(a.astype(BF16), b.astype(BF16), (((1,), (1,)), ((), ())),
                           preferred_element_type=F32)


def _dot_tn(a, b):
    return lax.dot_general(a.astype(BF16), b.astype(BF16), (((0,), (0,)), ((), ())),
                           preferred_element_type=F32)


def _dot32(a, b):
    return jnp.dot(a, b, precision=HI, preferred_element_type=F32)


def _dot32_nt(a, b):
    return lax.dot_general(a, b, (((1,), (1,)), ((), ())), precision=HI, preferred_element_type=F32)


def _split_dot(x, ones_mat):
    hi = x.astype(BF16)
    lo = (x - hi.astype(F32)).astype(BF16)
    return (jnp.dot(hi, ones_mat, preferred_element_type=F32)
            + jnp.dot(lo, ones_mat, preferred_element_type=F32))


def _sigmoid(x):
    return 1.0 / (1.0 + jnp.exp(-x))


def _silu(x):
    return x * _sigmoid(x)


def _softplus(x):
    return jnp.maximum(x, 0.0) + jnp.log1p(jnp.exp(-jnp.abs(x)))


def _gelu_tanh(x):
    return x * (0.5 * (1.0 + jnp.tanh(0.7978845608028654 * (x + 0.044715 * (x * x * x)))))


def _rms_rows(x, gain):
    ms = jnp.mean(x * x, axis=-1, keepdims=True)
    return x * lax.rsqrt(ms + NORM_EPS) * gain


def _norm_matmul_kernel(x_ref, g_ref, w_ref, o_ref, hn_ref):
    @pl.when(pl.program_id(1) == 0)
    def _():
        hn_ref[...] = _rms_rows(x_ref[...], g_ref[...]).astype(BF16)

    o_ref[...] = jnp.dot(hn_ref[...], w_ref[...], preferred_element_type=F32)


def _norm_matmul(x, gain, w, tm, tn):
    t, d = x.shape
    n = w.shape[1]
    return pl.pallas_call(
        _norm_matmul_kernel,
        grid=(t // tm, n // tn),
        in_specs=[pl.BlockSpec((tm, d), lambda i, j: (i, 0)),
                  pl.BlockSpec((1, d), lambda i, j: (0, 0)),
                  pl.BlockSpec((d, tn), lambda i, j: (0, j))],
        out_specs=pl.BlockSpec((tm, tn), lambda i, j: (i, j)),
        out_shape=jax.ShapeDtypeStruct((t, n), F32),
        scratch_shapes=[pltpu.VMEM((tm, d), BF16)],
        compiler_params=_params(("parallel", "arbitrary")),
        name="norm_matmul",
    )(x, gain, w)


def _matmul_residual_kernel(n_lhs, *refs):
    res_ref = refs[0]
    a_refs = refs[1:1 + n_lhs]
    w_refs = refs[1 + n_lhs:1 + 2 * n_lhs]
    o_ref = refs[1 + 2 * n_lhs]
    acc = res_ref[...]
    for a_ref, w_ref in zip(a_refs, w_refs):
        acc = acc + jnp.dot(a_ref[...].astype(BF16), w_ref[...], preferred_element_type=F32)
    o_ref[...] = acc


def _matmul_residual(res, lhs_list, w_list, tm):
    t, d = res.shape
    n_lhs = len(lhs_list)
    in_specs = [pl.BlockSpec((tm, d), lambda i: (i, 0))]
    in_specs += [pl.BlockSpec((tm, a.shape[1]), lambda i: (i, 0)) for a in lhs_list]
    in_specs += [pl.BlockSpec(w.shape, lambda i: (0, 0)) for w in w_list]
    return pl.pallas_call(
        functools.partial(_matmul_residual_kernel, n_lhs),
        grid=(t // tm,),
        in_specs=in_specs,
        out_specs=pl.BlockSpec((tm, d), lambda i: (i, 0)),
        out_shape=jax.ShapeDtypeStruct((t, d), F32),
        compiler_params=_params(("parallel",)),
        name="matmul_residual",
    )(res, *lhs_list, *w_list)


def _mlp_kernel(x_ref, g_ref, wu_ref, wd_ref, o_ref, hn_ref):
    f = pl.program_id(1)

    @pl.when(f == 0)
    def _():
        x = x_ref[...]
        hn_ref[...] = _rms_rows(x, g_ref[...]).astype(BF16)
        o_ref[...] = x

    h = jnp.dot(hn_ref[...], wu_ref[...], preferred_element_type=F32)
    h = jnp.maximum(h, 0.0)
    h = (h * h).astype(BF16)
    o_ref[...] += jnp.dot(h, wd_ref[...], preferred_element_type=F32)


def _mlp(x, gain, w_up, w_down, tm, tf):
    t, d = x.shape
    dff = w_up.shape[1]
    return pl.pallas_call(
        _mlp_kernel,
        grid=(t // tm, dff // tf),
        in_specs=[pl.BlockSpec((tm, d), lambda i, f: (i, 0)),
                  pl.BlockSpec((1, d), lambda i, f: (0, 0)),
                  pl.BlockSpec((d, tf), lambda i, f: (0, f)),
                  pl.BlockSpec((tf, d), lambda i, f: (f, 0))],
        out_specs=pl.BlockSpec((tm, d), lambda i, f: (i, 0)),
        out_shape=jax.ShapeDtypeStruct((t, d), F32),
        scratch_shapes=[pltpu.VMEM((tm, d), BF16)],
        compiler_params=_params(("parallel", "arbitrary")),
        name="mlp",
    )(x, gain, w_up, w_down)


def _gmlp_kernel(n_chunks, n_groups, u_ref, v_ref, vn_ref, wmix_ref, bias_ref, oa_ref, vout_ref):
    for c in range(n_chunks):
        rows = slice(c * GMLP_CHUNK, (c + 1) * GMLP_CHUNK)
        for g in range(n_groups):
            cols = slice(g * LANES, (g + 1) * LANES)
            vn = _rms_rows(_gelu_tanh(v_ref[rows, cols]), vn_ref[:, cols])
            vout_ref[rows, cols] = vn
            mixed = jnp.dot(wmix_ref[g], vn.astype(BF16), preferred_element_type=F32) + bias_ref[g]
            oa_ref[rows, cols] = _gelu_tanh(u_ref[rows, cols]) * mixed


def _gmlp(proj, u_blk, v_blk, v_norm, wmix, bias_full, rows_per_step):
    t = proj.shape[0]
    n_groups = wmix.shape[0]
    width = n_groups * LANES
    n_chunks = rows_per_step // GMLP_CHUNK
    return pl.pallas_call(
        functools.partial(_gmlp_kernel, n_chunks, n_groups),
        grid=(t // rows_per_step,),
        in_specs=[pl.BlockSpec((rows_per_step, width), lambda i: (i, u_blk)),
                  pl.BlockSpec((rows_per_step, width), lambda i: (i, v_blk)),
                  pl.BlockSpec((1, width), lambda i: (0, 0)),
                  pl.BlockSpec(wmix.shape, lambda i: (0, 0, 0)),
                  pl.BlockSpec(bias_full.shape, lambda i: (0, 0, 0))],
        out_specs=[pl.BlockSpec((rows_per_step, width), lambda i: (i, 0)),
                   pl.BlockSpec((rows_per_step, width), lambda i: (i, 0))],
        out_shape=[jax.ShapeDtypeStruct((t, width), F32), jax.ShapeDtypeStruct((t, width), F32)],
        compiler_params=_params(("parallel",)),
        name="gmlp",
    )(proj, proj, v_norm, wmix, bias_full)


def _l2norm_rows(x):
    return x * lax.rsqrt(jnp.sum(x * x, axis=-1, keepdims=True) + NORM_EPS)


def _gdn_gates(sm, alog, dtb):
    beta = _sigmoid(sm[:, :LANES])
    g = -jnp.exp(alog) * _softplus(sm[:, LANES:] + dtb)
    return beta, g


def _gdn_out(o, onorm, gate):
    return _rms_rows(o, onorm) * _silu(gate)


def _transpose_rows(x):
    rows = x.shape[0]
    if rows < LANES:
        x = jnp.concatenate([x, jnp.zeros((LANES - rows, LANES), F32)], axis=0)
    return jnp.transpose(x)


def _split3(x):
    hi = x.astype(BF16)
    return hi, (x - hi.astype(F32)).astype(BF16)


def _dot3(a, b, nt=False):
    dims = (((1,), (1,)), ((), ())) if nt else (((1,), (0,)), ((), ()))
    rows = a[0].shape[0]
    both = lax.dot_general(jnp.concatenate([a[0], a[1]], axis=0), b[0], dims, preferred_element_type=F32)
    return both[:rows] + both[rows:] + lax.dot_general(a[0], b[1], dims, preferred_element_type=F32)


def _block_diag(parts, n, mask):
    return tuple(jnp.concatenate([p] * n, axis=0) * mask for p in parts)


def _fold_diag_blocks(full, n, blk):
    r = lax.shift_right_logical(lax.broadcasted_iota(jnp.int32, full.shape, 0), blk.bit_length() - 1)
    c = lax.shift_right_logical(lax.broadcasted_iota(jnp.int32, full.shape, 1), blk.bit_length() - 1)
    kept = jnp.where(r == c, full, 0.0)
    out = kept[0:blk]
    for i in range(1, n):
        out = out + kept[i * blk:(i + 1) * blk]
    return out


def _inv_unit_lower_cat(l_cat, eye_cat, n, mask):
    size = l_cat.shape[0]
    x_bd = _block_diag(_split3(-l_cat), n, mask)
    x = -l_cat
    p = jnp.where(eye_cat, 1.0, 0.0) - l_cat
    m = 2
    while m < size:
        x = _dot3(_split3(x), x_bd)
        x_bd = _block_diag(_split3(x), n, mask)
        p = p + _dot3(_split3(p), x_bd)
        m *= 2
    return p


def _gdn_prompt_kernel(n_batch, n_heads, chunk, x_ref, gate_ref, sm_ref, cw_ref, alog_ref, dtb_ref, onorm_ref,
                       bd_ref, bd2_ref, o_ref, s_out_ref, xbuf_ref, s_ref):
    c_idx = pl.program_id(0)
    hw = n_heads * LANES
    cw = n_heads * chunk
    step_rows = x_ref.shape[1]
    n_chunks = step_rows // chunk

    @pl.when(c_idx == 0)
    def _():
        xbuf_ref[:, 0:SUBLANES, :] = jnp.zeros((n_batch, SUBLANES, xbuf_ref.shape[2]), F32)
        s_ref[...] = jnp.zeros(s_ref.shape, F32)

    r = lax.broadcasted_iota(jnp.int32, (chunk, chunk), 0)
    c = lax.broadcasted_iota(jnp.int32, (chunk, chunk), 1)
    incl_ones = jnp.where(c <= r, 1.0, 0.0)
    row_i = lax.broadcasted_iota(jnp.int32, (chunk, cw), 0)
    lane = lax.broadcasted_iota(jnp.int32, (chunk, cw), 1)
    lane_j = lane & (chunk - 1)
    lane_h = lax.shift_right_logical(lane, chunk.bit_length() - 1)
    incl_cat = lane_j <= row_i
    strict_cat = lane_j < row_i
    eye_cat = lane_j == row_i
    bd_mask = bd_ref[...]
    bd2_mask = bd2_ref[...]

    for b in range(n_batch):
        x = x_ref[b]
        xbuf_ref[b, SUBLANES:SUBLANES + step_rows, :] = x
        conv = cw_ref[CONV_K - 1:CONV_K, :] * x
        for i in range(CONV_K - 1):
            shift = CONV_K - 1 - i
            conv = conv + cw_ref[i:i + 1, :] * xbuf_ref[b, SUBLANES - shift:SUBLANES - shift + step_rows, :]
        xbuf_ref[b, 0:SUBLANES, :] = x[step_rows - SUBLANES:step_rows, :]
        act_all = _silu(conv)
        beta_all, g_all = _gdn_gates(sm_ref[b], alog_ref[...], dtb_ref[...])

        for cc in range(n_chunks):
            rows = slice(cc * chunk, (cc + 1) * chunk)
            act = act_all[rows]
            beta = beta_all[rows]
            g = g_all[rows]
            heads = range(n_heads)
            q = [_l2norm_rows(act[:, h * LANES:(h + 1) * LANES]) * (LANES ** -0.5) for h in heads]
            k = [_l2norm_rows(act[:, hw + h * LANES:hw + (h + 1) * LANES]) for h in heads]
            v = [act[:, 2 * hw + h * LANES:2 * hw + (h + 1) * LANES] for h in heads]
            b_col = [beta[:, h:h + 1] for h in heads]
            kb = [k[h] * b_col[h] for h in heads]

            g_cat = jnp.broadcast_to(g[:, 0:1], (chunk, cw))
            for h in range(1, n_heads):
                g_cat = jnp.where(lane_h == h, g[:, h:h + 1], g_cat)
            gcol_cat = _dot32(incl_ones, g_cat)
            grow_cat = jnp.sum(jnp.where(lane_j >= row_i, g_cat, 0.0), axis=0, keepdims=True)
            decay_cat = jnp.where(incl_cat, jnp.exp(jnp.where(incl_cat, gcol_cat - grow_cat, 0.0)), 0.0)
            gc_col = [gcol_cat[:, h * chunk:h * chunk + 1] for h in heads]
            egc = [jnp.exp(gc_col[h]) for h in heads]
            g_last = [gc_col[h][chunk - 1:chunk] for h in heads]

            k_stack = jnp.concatenate(k, axis=0)
            kk = _fold_diag_blocks(_dot3(_split3(jnp.concatenate(kb, axis=0)), _split3(k_stack), nt=True),
                                   n_heads, chunk)
            qk = _fold_diag_blocks(_dot_nt(jnp.concatenate(q, axis=0), k_stack), n_heads, chunk)
            t_cat = _inv_unit_lower_cat(jnp.where(strict_cat, kk * decay_cat, 0.0), eye_cat, n_heads, bd_mask)
            a_in = (qk * decay_cat).astype(BF16)

            rhs = jnp.concatenate([v[h] * b_col[h] for h in heads] + [kb[h] * egc[h] for h in heads], axis=1)
            uw = jnp.dot(t_cat.astype(BF16), jnp.concatenate([rhs.astype(BF16)] * n_heads, axis=0) * bd2_mask,
                         preferred_element_type=F32)

            s_old = [s_ref[b, h] for h in heads]
            v_new, o_inter = [], []
            for h in heads:
                wq = _dot(jnp.concatenate([uw[:, hw + h * LANES:hw + (h + 1) * LANES], q[h] * egc[h]], axis=0),
                          s_old[h])
                v_new.append(uw[:, h * LANES:(h + 1) * LANES] - wq[:chunk])
                o_inter.append(wq[chunk:])
            vn_cat = jnp.concatenate(v_new, axis=1).astype(BF16)
            o_intra = jnp.dot(a_in, jnp.concatenate([vn_cat] * n_heads, axis=0) * bd2_mask[:, :hw],
                              preferred_element_type=F32)
            for h in heads:
                cols = slice(h * LANES, (h + 1) * LANES)
                k_dec = k[h] * jnp.exp(g_last[h] - gc_col[h])
                v_pad = jnp.concatenate([v_new[h], jnp.zeros((LANES - chunk, LANES), F32)], axis=0)
                s_ref[b, h] = s_old[h] * jnp.exp(g_last[h]) + _dot(_transpose_rows(k_dec), v_pad)
                o_ref[b, rows, cols] = _gdn_out(o_inter[h] + o_intra[:, cols], onorm_ref[...],
                                                gate_ref[b, rows, cols])

    @pl.when(c_idx == pl.num_programs(0) - 1)
    def _():
        s_out_ref[...] = s_ref[...]


def _gdn_prompt(proj, n_batch, seq, n_heads, gate_blk, sm_blk, conv_w, alog, dtb, onorm):
    chunk = GDN_CHUNK
    step = min(GDN_CHUNKS_PER_STEP * chunk, seq)
    assert seq % step == 0 and step % chunk == 0
    qkv_w = 3 * n_heads * LANES
    hw = n_heads * LANES
    cw = n_heads * chunk
    proj3 = proj.reshape(n_batch, seq, proj.shape[1])
    blk_row = (jnp.arange(cw) // chunk)[:, None]
    bd_mask = (blk_row == (jnp.arange(cw) // chunk)[None, :]).astype(BF16)
    bd2_mask = (blk_row == ((jnp.arange(2 * hw) // LANES) % n_heads)[None, :]).astype(BF16)
    const = lambda a: pl.BlockSpec(a.shape, lambda c: (0,) * a.ndim)
    o, s_fin = pl.pallas_call(
        functools.partial(_gdn_prompt_kernel, n_batch, n_heads, chunk),
        grid=(seq // step,),
        in_specs=[pl.BlockSpec((n_batch, step, qkv_w), lambda c: (0, c, 0)),
                  pl.BlockSpec((n_batch, step, hw), lambda c: (0, c, gate_blk)),
                  pl.BlockSpec((n_batch, step, 2 * LANES), lambda c: (0, c, sm_blk)),
                  const(conv_w), const(alog), const(dtb), const(onorm), const(bd_mask), const(bd2_mask)],
        out_specs=[pl.BlockSpec((n_batch, step, hw), lambda c: (0, c, 0)),
                   pl.BlockSpec((n_batch, n_heads, LANES, LANES), lambda c: (0, 0, 0, 0))],
        out_shape=[jax.ShapeDtypeStruct((n_batch, seq, hw), F32),
                   jax.ShapeDtypeStruct((n_batch, n_heads, LANES, LANES), F32)],
        scratch_shapes=[pltpu.VMEM((n_batch, SUBLANES + step, qkv_w), F32),
                        pltpu.VMEM((n_batch, n_heads, LANES, LANES), F32)],
        compiler_params=_params(("arbitrary",)),
        name="gdn_prompt",
    )(proj3, proj3, proj3, conv_w, alog, dtb, onorm, bd_mask, bd2_mask)
    return o.reshape(n_batch * seq, hw), s_fin


def _gdn_sample_kernel(n_heads, n_tok, bb, xp_ref, gate_ref, sm_ref, cw_ref, alog_ref, dtb_ref, onorm_ref,
                       s0_ref, o_ref, s_out_ref):
    hw = n_heads * LANES

    def one_request(b, carry):
        conv = cw_ref[0:1, :] * xp_ref[b, 0:PAD_TOK, :]
        for i in range(1, CONV_K):
            conv = conv + cw_ref[i:i + 1, :] * xp_ref[b, i:i + PAD_TOK, :]
        act = _silu(conv)
        beta, g = _gdn_gates(sm_ref[b], alog_ref[...], dtb_ref[...])
        eg = jnp.exp(g)
        gate = gate_ref[b]
        ks = [_l2norm_rows(act[:, hw + h * LANES:hw + (h + 1) * LANES]) for h in range(n_heads)]
        qs = [_l2norm_rows(act[:, h * LANES:(h + 1) * LANES]) * (LANES ** -0.5) for h in range(n_heads)]
        kq_t = _transpose_rows(jnp.concatenate(ks + qs, axis=0))
        for h in range(n_heads):
            v_h = act[:, 2 * hw + h * LANES:2 * hw + (h + 1) * LANES]
            s = s0_ref[b, h]
            outs = []
            for t in range(n_tok):
                kc = kq_t[:, h * PAD_TOK + t:h * PAD_TOK + t + 1]
                qc = kq_t[:, (n_heads + h) * PAD_TOK + t:(n_heads + h) * PAD_TOK + t + 1]
                e = eg[t:t + 1, h:h + 1]
                ks = jnp.sum(s * kc, axis=0, keepdims=True)
                d = beta[t:t + 1, h:h + 1] * (v_h[t:t + 1, :] - e * ks)
                s = e * s + kc * d
                outs.append(jnp.sum(s * qc, axis=0, keepdims=True))
            outs.append(jnp.zeros((PAD_TOK - n_tok, LANES), F32))
            o = jnp.concatenate(outs, axis=0)
            s_out_ref[b, h] = s
            o_ref[b, :, h * LANES:(h + 1) * LANES] = _gdn_out(
                o, onorm_ref[...], gate[:, h * LANES:(h + 1) * LANES])
        return carry

    lax.fori_loop(0, bb, one_request, 0)


def _gdn_sample(xp, gate, sm, conv_w, alog, dtb, onorm, s0, n_tok, bb):
    nb, n_heads = s0.shape[0], s0.shape[1]
    hw = n_heads * LANES
    qkv_w = 3 * hw
    return pl.pallas_call(
        functools.partial(_gdn_sample_kernel, n_heads, n_tok, bb),
        grid=(nb // bb,),
        in_specs=[pl.BlockSpec((bb, 2 * PAD_TOK, qkv_w), lambda i: (i, 0, 0)),
                  pl.BlockSpec((bb, PAD_TOK, hw), lambda i: (i, 0, 0)),
                  pl.BlockSpec((bb, PAD_TOK, 2 * LANES), lambda i: (i, 0, 0)),
                  pl.BlockSpec((CONV_K, qkv_w), lambda i: (0, 0)),
                  pl.BlockSpec((1, LANES), lambda i: (0, 0)),
                  pl.BlockSpec((1, LANES), lambda i: (0, 0)),
                  pl.BlockSpec((1, LANES), lambda i: (0, 0)),
                  pl.BlockSpec((bb, n_heads, LANES, LANES), lambda i: (i, 0, 0, 0))],
        out_specs=[pl.BlockSpec((bb, PAD_TOK, hw), lambda i: (i, 0, 0)),
                   pl.BlockSpec((bb, n_heads, LANES, LANES), lambda i: (i, 0, 0, 0))],
        out_shape=[jax.ShapeDtypeStruct((nb, PAD_TOK, hw), F32),
                   jax.ShapeDtypeStruct(s0.shape, F32)],
        compiler_params=_params(("parallel",)),
        name="gdn_sample",
    )(xp, gate, sm, conv_w, alog, dtb, onorm, s0)


def _head_rms(r, gain, seg_ones, head_dim):
    ss = _split_dot(r * r, seg_ones)
    return r * lax.rsqrt(ss * (1.0 / head_dim) + NORM_EPS) * gain


def _sb_qkv_kernel(head_dim, n_prev, x_ref, g_ref, w_ref, qn_ref, kn_ref, seg_ref, *refs):
    qb_ref, k_ref, v_ref, kb_ref, vb_ref, hn_ref = refs[n_prev:]
    j = pl.program_id(1)

    @pl.when(j == 0)
    def _():
        hn_ref[...] = _rms_rows(x_ref[...], g_ref[...]).astype(BF16)

    r = jnp.dot(hn_ref[...], w_ref[...], preferred_element_type=F32)
    n_blk = r.shape[1] // LANES
    seg = seg_ref[...]

    @pl.when(j == 0)
    def _():
        for c in range(n_blk):
            cols = slice(c * LANES, (c + 1) * LANES)
            qn = _head_rms(r[:, cols], qn_ref[...], seg, head_dim)
            qb_ref[:, cols] = (qn * (head_dim ** -0.5)).astype(BF16)

    @pl.when(j == 1)
    def _():
        for c in range(n_blk):
            cols = slice(c * LANES, (c + 1) * LANES)
            kn = _head_rms(r[:, cols], kn_ref[...], seg, head_dim)
            k_ref[:, cols] = kn
            kb_ref[:, cols] = kn.astype(BF16)

    @pl.when(j == 2)
    def _():
        v_ref[...] = r
        vb_ref[...] = r.astype(BF16)


def _sb_qkv(x, gain, w, q_gain, k_gain, seg_ones, head_dim, tm, layer, n_layers, kv_stacks):
    t, d = x.shape
    width = w.shape[1] // 3
    row_blk = pl.BlockSpec((tm, width), lambda i, j: (i, 0))
    stack_blk = pl.BlockSpec((None, tm, width), lambda i, j: (layer, i, 0))
    n_fixed = 6
    return pl.pallas_call(
        functools.partial(_sb_qkv_kernel, head_dim, len(kv_stacks)),
        grid=(t // tm, 3),
        in_specs=[pl.BlockSpec((tm, d), lambda i, j: (i, 0)),
                  pl.BlockSpec((1, d), lambda i, j: (0, 0)),
                  pl.BlockSpec((d, width), lambda i, j: (0, j)),
                  pl.BlockSpec((1, LANES), lambda i, j: (0, 0)),
                  pl.BlockSpec((1, LANES), lambda i, j: (0, 0)),
                  pl.BlockSpec((LANES, LANES), lambda i, j: (0, 0))]
        + [pl.BlockSpec(memory_space=pl.ANY) for _ in kv_stacks],
        out_specs=[row_blk, stack_blk, stack_blk, row_blk, row_blk],
        out_shape=[jax.ShapeDtypeStruct((t, width), BF16),
                   jax.ShapeDtypeStruct((n_layers, t, width), F32),
                   jax.ShapeDtypeStruct((n_layers, t, width), F32),
                   jax.ShapeDtypeStruct((t, width), BF16),
                   jax.ShapeDtypeStruct((t, width), BF16)],
        input_output_aliases={n_fixed + s: 1 + s for s in range(len(kv_stacks))},
        scratch_shapes=[pltpu.VMEM((tm, d), BF16)],
        compiler_params=_params(("parallel", "arbitrary")),
        name="sb_qkv",
    )(x, gain, w, q_gain, k_gain, seg_ones, *kv_stacks)


def _attn_prompt_kernel(head_dim, bias_ref, q_ref, k_ref, v_ref, suf_ref, o_ref,
                        acc_ref, carry_ref, z_ref, cs_ref, rs_ref):
    g = pl.program_id(1)
    qi = pl.program_id(2)
    tq = q_ref.shape[0]
    tk = suf_ref.shape[0]
    n_pairs = q_ref.shape[1] // LANES
    rows = 2 * tq
    lane = lax.broadcasted_iota(jnp.int32, (tq, LANES), 1)
    first = lane < head_dim
    row1 = lax.broadcasted_iota(jnp.int32, (rows, 1), 0)
    qq, bias = [], []
    for pp in range(n_pairs):
        q = q_ref[:, pp * LANES:(pp + 1) * LANES].astype(F32)
        qq.append(jnp.concatenate([jnp.where(first, q, 0.0), jnp.where(first, 0.0, q)], axis=0).astype(BF16))
        head0 = 2 * (g * n_pairs + pp)
        bias.append(jnp.where(row1 < tq, bias_ref[head0], bias_ref[head0 + 1]))
    acc_ref[...] = jnp.zeros(acc_ref.shape, F32)
    carry_ref[...] = jnp.zeros(carry_ref.shape, F32)

    def causal_mask():
        r2 = lax.broadcasted_iota(jnp.int32, (rows, tk), 0)
        c2 = lax.broadcasted_iota(jnp.int32, (rows, tk), 1)
        return c2 < jnp.where(r2 >= tq, r2 - tq, r2)

    def logits(j, zs):
        start = pl.multiple_of(j * tk, tk)
        for pp in range(n_pairs):
            kblk = k_ref[pl.ds(start, tk), pp * LANES:(pp + 1) * LANES]
            z_ref[pp, zs] = lax.dot_general(qq[pp], kblk, (((1,), (1,)), ((), ())),
                                            preferred_element_type=F32) + bias[pp]

    def keeps(zs, cs_slot, masked):
        for pp in range(n_pairs):
            z = z_ref[pp, zs]
            lk = jnp.minimum(-z, 0.0) - jnp.log(1.0 + jnp.exp2(jnp.abs(z) * (-LOG2E)))
            if masked:
                lk = jnp.where(causal_mask(), lk, 0.0)
            cs_ref[pp, cs_slot] = jnp.dot(lk.astype(BF16), suf_ref[...], preferred_element_type=F32)
            rs_ref[pp, cs_slot] = jnp.broadcast_to(jnp.sum(lk, axis=1, keepdims=True), (rows, LANES))

    def weigh(j, zs, cs_slot, masked, live=None):
        start = pl.multiple_of(j * tk, tk)
        for pp in range(n_pairs):
            vblk = v_ref[pl.ds(start, tk), pp * LANES:(pp + 1) * LANES]
            carry = carry_ref[pp]
            a = jnp.exp(z_ref[pp, zs] + cs_ref[pp, cs_slot] + jnp.concatenate([carry] * (tk // LANES), axis=1))
            if masked:
                a = jnp.where(causal_mask(), a, 0.0)
            rs = rs_ref[pp, cs_slot]
            if live is not None:
                a = jnp.where(live, a, 0.0)
                rs = jnp.where(live, rs, 0.0)
            acc_ref[pp] += jnp.dot(a.astype(BF16), vblk, preferred_element_type=F32)
            carry_ref[pp] = carry + rs

    def next3(s):
        return jnp.where(s == 2, 0, s + 1)

    logits(qi, 2)
    logits(jnp.maximum(qi - 1, 0), 0)
    keeps(2, 1, True)
    logits(jnp.maximum(qi - 2, 0), 1)
    weigh(qi, 2, 1, True)
    keeps(0, 0, False)

    def trip(j, zc, cc):
        zb = next3(zc)
        weigh(j, zc, cc, False)
        keeps(zb, 1 - cc, False)
        logits(j - 2, next3(zb))
        return zb, 1 - cc

    def unrolled_trips(t, slots):
        for u in range(ATT_TRIPS_PER_BODY):
            slots = trip(qi - 1 - ATT_TRIPS_PER_BODY * t - u, *slots)
        return slots

    n_trips = jnp.maximum(qi - 2, 0)
    n_bodies = lax.shift_right_logical(n_trips, ATT_TRIPS_PER_BODY.bit_length() - 1)
    slots = lax.fori_loop(0, n_bodies, unrolled_trips, (0, 0))
    first_left = qi - 1 - ATT_TRIPS_PER_BODY * n_bodies
    zc, cc = lax.fori_loop(0, n_trips & (ATT_TRIPS_PER_BODY - 1),
                           lambda r, s: trip(first_left - r, *s), slots)

    zb = next3(zc)
    weigh(jnp.minimum(qi, 1), zc, cc, False, live=qi >= 2)
    keeps(zb, 1 - cc, False)
    weigh(0, zb, 1 - cc, False, live=qi >= 1)

    for pp in range(n_pairs):
        acc = acc_ref[pp]
        o_ref[:, pp * LANES:(pp + 1) * LANES] = jnp.where(first, acc[:tq], acc[tq:])


def _attn_prompt(qb, kb, vb, bias, suffix_ones, n_batch, seq, head_dim):
    width = qb.shape[1]
    tq = suffix_ones.shape[0]
    nq = seq // tq
    npp = ATT_PAIRS_PER_STEP
    blk_w = npp * LANES
    assert LANES == 2 * head_dim and seq % tq == 0 and width % blk_w == 0
    return pl.pallas_call(
        functools.partial(_attn_prompt_kernel, head_dim),
        grid=(n_batch, width // blk_w, nq),
        in_specs=[pl.BlockSpec(memory_space=pltpu.SMEM),
                  pl.BlockSpec((tq, blk_w), lambda b, p, i: (b * nq + i, p)),
                  pl.BlockSpec((seq, blk_w), lambda b, p, i: (b, p)),
                  pl.BlockSpec((seq, blk_w), lambda b, p, i: (b, p)),
                  pl.BlockSpec(suffix_ones.shape, lambda b, p, i: (0, 0))],
        out_specs=pl.BlockSpec((tq, blk_w), lambda b, p, i: (b * nq + i, p)),
        out_shape=jax.ShapeDtypeStruct((n_batch * seq, width), F32),
        scratch_shapes=[pltpu.VMEM((npp, 2 * tq, LANES), F32), pltpu.VMEM((npp, 2 * tq, LANES), F32),
                        pltpu.VMEM((npp, 3, 2 * tq, tq), F32), pltpu.VMEM((npp, 2, 2 * tq, tq), F32),
                        pltpu.VMEM((npp, 2, 2 * tq, LANES), F32)],
        compiler_params=_params(("parallel", "parallel", "arbitrary")),
        name="attn_prompt",
    )(bias, qb, kb, vb, suffix_ones)


def _attn_sample_kernel(head_dim, n_tok, n_kv, pt_ref, qbd_ref, kn_ref, vn_ref, *refs):
    kv_refs = refs[:n_kv]
    bias_ref, suf_ref, o_ref, acc_ref, carry_ref = refs[n_kv:]
    j = pl.program_id(1)
    n_rows, width = acc_ref.shape
    row_head = lax.shift_right_logical(lax.broadcasted_iota(jnp.int32, (n_rows, width), 0),
                                       n_tok.bit_length() - 1)
    col_head = lax.shift_right_logical(lax.broadcasted_iota(jnp.int32, (n_rows, width), 1),
                                       head_dim.bit_length() - 1)
    own = row_head == col_head
    bias = bias_ref[...]
    qbd = qbd_ref[0]

    @pl.when(j == 0)
    def _():
        qf = qbd.astype(F32)
        tok = lax.broadcasted_iota(jnp.int32, (n_rows, 1), 0) & (n_tok - 1)
        bcol = bias[:, 0:1]
        lks, lss, valids = [], [], []
        for s in range(n_tok):
            z = jnp.sum(qf * kn_ref[0, s:s + 1, :], axis=-1, keepdims=True) + bcol
            sp = _softplus(z)
            valid = tok > s
            valids.append(valid)
            lks.append(jnp.where(valid, -sp, 0.0))
            lss.append(z - sp)
        later = jnp.zeros((n_rows, 1), F32)
        acc = jnp.zeros((n_rows, width), F32)
        for s in reversed(range(n_tok)):
            a = jnp.where(valids[s], jnp.exp(lss[s] + later), 0.0)
            acc = acc + a * vn_ref[0, s:s + 1, :]
            later = later + lks[s]
        acc_ref[...] = acc
        carry_ref[...] = jnp.broadcast_to(later, carry_ref.shape)

    k_refs, v_refs = kv_refs[0::2], kv_refs[1::2]
    zs = [jnp.dot(qbd, k_ref[...].astype(BF16), preferred_element_type=F32) + bias for k_ref in k_refs]
    lks = [jnp.minimum(-z, 0.0) - jnp.log(1.0 + jnp.exp2(jnp.abs(z) * (-LOG2E))) for z in zs]
    css = [jnp.dot(lk.astype(BF16), suf_ref[...], preferred_element_type=F32) for lk in lks]
    carry = carry_ref[...]
    acc = acc_ref[...]
    for z, lk, cs, v_ref in zip(zs, lks, css, v_refs):
        a = jnp.exp(z + cs + carry)
        acc = acc + lax.dot_general(a.astype(BF16), v_ref[...].astype(BF16), (((1,), (1,)), ((), ())),
                                    preferred_element_type=F32)
        carry = carry + jnp.sum(lk, axis=1, keepdims=True)
    carry_ref[...] = carry
    acc_ref[...] = acc

    @pl.when(j == pl.num_programs(1) - 1)
    def _():
        own_acc = jnp.where(own, acc_ref[...], 0.0)
        sel_r = lax.broadcasted_iota(jnp.int32, (SUBLANES, n_rows), 0)
        sel_c = lax.broadcasted_iota(jnp.int32, (SUBLANES, n_rows), 1) & (n_tok - 1)
        o = _dot32(jnp.where(sel_r == sel_c, 1.0, 0.0), own_acc)
        o_ref[0] = o[:n_tok]


def _attn_sample(page_table, qbd, k_new, v_new, cache_k, cache_v, layer, bias_rows, suffix_ones,
                 head_dim, n_tok):
    nb, n_pages = page_table.shape
    n_layers, n_pool, page, n_heads, _ = cache_k.shape
    width = n_heads * head_dim
    ck = jnp.transpose(cache_k, (0, 1, 3, 4, 2)).reshape(n_layers * n_pool, width, page)
    cv = jnp.transpose(cache_v, (0, 1, 3, 4, 2)).reshape(n_layers * n_pool, width, page)
    n_rows = n_heads * n_tok
    base = layer * n_pool
    per_step = min(SAMPLE_PAGES_PER_STEP, n_pages)
    assert n_pages % per_step == 0 and n_tok & (n_tok - 1) == 0 and n_tok <= SUBLANES

    def page_spec(g):
        def page_idx(b, j, pt):
            return (base + pt[b * n_pages + (n_pages - 1 - (j * per_step + g))], 0, 0)
        return pl.BlockSpec((None, width, page), page_idx)

    kv_specs, kv_args = [], []
    for g in range(per_step):
        kv_specs += [page_spec(g), page_spec(g)]
        kv_args += [ck, cv]
    tok_blk = pl.BlockSpec((1, n_tok, width), lambda b, j, pt: (b, 0, 0))
    grid_spec = pltpu.PrefetchScalarGridSpec(
        num_scalar_prefetch=1,
        grid=(nb, n_pages // per_step),
        in_specs=[pl.BlockSpec((1, n_rows, width), lambda b, j, pt: (b, 0, 0)), tok_blk, tok_blk] + kv_specs + [
            pl.BlockSpec((n_rows, LANES), lambda b, j, pt: (0, 0)),
            pl.BlockSpec(suffix_ones.shape, lambda b, j, pt: (0, 0))],
        out_specs=tok_blk,
        scratch_shapes=[pltpu.VMEM((n_rows, width), F32),
                        pltpu.VMEM((n_rows, LANES), F32)],
    )
    return pl.pallas_call(
        functools.partial(_attn_sample_kernel, head_dim, n_tok, 2 * per_step),
        grid_spec=grid_spec,
        out_shape=jax.ShapeDtypeStruct((nb, n_tok, width), F32),
        compiler_params=_params(("parallel", "arbitrary")),
        name="attn_sample",
    )(page_table.reshape(-1), qbd, k_new, v_new, *kv_args, bias_rows, suffix_ones)


def _pad_lanes(x, width):
    return jnp.pad(x, [(0, 0)] * (x.ndim - 1) + [(0, width - x.shape[-1])])


def _pad_tokens(x, total):
    return jnp.pad(x, [(0, 0), (0, total - x.shape[1]), (0, 0)])


def kernel(x_prompt, x_sample, cache_k, cache_v, state_delta, state_conv, page_table, norm_mix, norm_mlp,
           ab_w_in, ab_w_out, gmlp_w_s, gmlp_b_s, gmlp_v_norm, gdn_conv_w, gdn_a_log, gdn_dt_bias,
           gdn_out_norm, sb_w_qkv, sb_w_out, sb_q_norm, sb_k_norm, sb_logit_bias, mlp_w_up, mlp_w_down):
    n_batch, seq, d_model = x_prompt.shape
    nb_s, n_tok, _ = x_sample.shape
    depth = norm_mix.shape[0]
    n_groups = gmlp_w_s.shape[1]
    a_width = n_groups * LANES
    n_bheads = state_delta.shape[2]
    b_qkv = state_conv.shape[-1]
    b_width = n_bheads * LANES
    c_heads, head_dim = cache_k.shape[3], cache_k.shape[4]
    n_prompt = n_batch * seq
    n_sample = nb_s * n_tok
    assert a_width == b_width and b_qkv == 3 * b_width and LANES % head_dim == 0
    assert seq % ROW_TILE == 0 and n_sample % ROW_TILE == 0 and n_sample % GMLP_CHUNK == 0
    assert GMLP_CHUNK % n_tok == 0 and n_tok <= PAD_TOK

    y_p = x_prompt.reshape(n_prompt, d_model)
    y_s = x_sample.reshape(n_sample, d_model)
    assert cache_k.shape[2] == PAGE_BLOCK
    pos = jnp.arange(PAGE_BLOCK)
    page_suffix_ones = (pos[:, None] >= pos[None, :]).astype(BF16)
    blk = jnp.arange(min(ATT_BLOCK, seq))
    suffix_ones = (blk[:, None] >= blk[None, :]).astype(BF16)
    seg = jnp.arange(LANES) // head_dim
    seg_ones = (seg[:, None] == seg[None, :]).astype(BF16)

    n_c_layers = depth // 2
    kv_pr, kv_sa = (), ()
    d_pr, d_sa, c_pr, c_sa, gv_sa = [], [], [], [], []
    for layer in range(depth):
        li = layer // 2
        gain_mix = norm_mix[layer][None, :]
        if layer % 2 == 0:
            w_in = ab_w_in[li]
            o = 0
            a_u, o = w_in[:, o:o + a_width], o + a_width
            a_v, o = w_in[:, o:o + a_width], o + a_width
            qkv_w, o = w_in[:, o:o + b_qkv], o + b_qkv
            beta_w, o = w_in[:, o:o + n_bheads], o + n_bheads
            araw_w, o = w_in[:, o:o + n_bheads], o + n_bheads
            gate_w = w_in[:, o:]
            w_cat = jnp.concatenate([qkv_w, a_u, a_v, gate_w, _pad_lanes(beta_w, LANES),
                                     _pad_lanes(araw_w, LANES)], axis=1).astype(BF16)
            u_blk = b_qkv // a_width
            v_blk = u_blk + 1
            gate_blk = u_blk + 2
            sm_blk = (b_qkv + 3 * a_width) // (2 * LANES)
            tn = w_cat.shape[1] // 2
            w_out = ab_w_out[li].astype(BF16)
            alog = _pad_lanes(gdn_a_log[li][None, :], LANES)
            dtb = _pad_lanes(gdn_dt_bias[li][None, :], LANES)
            onorm = gdn_out_norm[li][None, :]
            conv_w = gdn_conv_w[li]
            v_norm = gmlp_v_norm[li][None, :]
            tril = jnp.tril(jnp.ones((GMLP_CHUNK, GMLP_CHUNK), bool))
            wmix_p = jnp.where(tril, gmlp_w_s[li], 0.0).astype(BF16)
            bias_p = jnp.broadcast_to(gmlp_b_s[li][:, :, None], (n_groups, GMLP_CHUNK, LANES))
            small = jnp.where(tril[:n_tok, :n_tok], gmlp_w_s[li][:, :n_tok, :n_tok], 0.0)
            eye_req = jnp.eye(GMLP_CHUNK // n_tok, dtype=F32)
            wmix_s = jnp.einsum('ab,gij->gaibj', eye_req, small).reshape(
                n_groups, GMLP_CHUNK, GMLP_CHUNK).astype(BF16)
            bias_s = jnp.broadcast_to(
                jnp.tile(gmlp_b_s[li][:, :n_tok], (1, GMLP_CHUNK // n_tok))[:, :, None],
                (n_groups, GMLP_CHUNK, LANES))

            proj_p = _norm_matmul(y_p, gain_mix, w_cat, ROW_TILE, tn)
            oa_p, _ = _gmlp(proj_p, u_blk, v_blk, v_norm, wmix_p, bias_p, ROW_TILE)
            ob_p, s_p = _gdn_prompt(proj_p, n_batch, seq, n_bheads, gate_blk, sm_blk, conv_w, alog, dtb, onorm)
            c_pr.append(proj_p[:, :b_qkv].reshape(n_batch, seq, b_qkv)[:, seq - (CONV_K - 1):])
            d_pr.append(s_p)
            y_p = _matmul_residual(y_p, [oa_p, ob_p], [w_out[:a_width], w_out[a_width:]], ROW_TILE)

            proj_s = _norm_matmul(y_s, gain_mix, w_cat, ROW_TILE, tn)
            oa_s, vn_s = _gmlp(proj_s, u_blk, v_blk, v_norm, wmix_s, bias_s, ROW_TILE)
            proj_s3 = proj_s.reshape(nb_s, n_tok, -1)
            qkv_s = proj_s3[:, :, :b_qkv]
            xp = _pad_tokens(jnp.concatenate([state_conv[li], qkv_s], axis=1), 2 * PAD_TOK)
            gate_s = _pad_tokens(proj_s3[:, :, gate_blk * a_width:(gate_blk + 1) * a_width], PAD_TOK)
            sm_s = _pad_tokens(proj_s3[:, :, sm_blk * 2 * LANES:], PAD_TOK)
            ob_s, s_s = _gdn_sample(xp, gate_s, sm_s, conv_w, alog, dtb, onorm, state_delta[li], n_tok, 8)
            ob_s = ob_s[:, :n_tok].reshape(n_sample, b_width)
            c_sa.append(qkv_s[:, n_tok - (CONV_K - 1):])
            d_sa.append(s_s)
            gv_sa.append(vn_s.reshape(nb_s, n_tok, a_width))
            y_s = _matmul_residual(y_s, [oa_s, ob_s], [w_out[:a_width], w_out[a_width:]], ROW_TILE)
        else:
            w_qkv = sb_w_qkv[li].astype(BF16)
            w_out = sb_w_out[li].astype(BF16)
            width = c_heads * head_dim
            q_gain = jnp.tile(sb_q_norm[li], LANES // head_dim)[None, :]
            k_gain = jnp.tile(sb_k_norm[li], LANES // head_dim)[None, :]
            bias = sb_logit_bias[li]

            qb, k_pr, v_pr, kb, vb = _sb_qkv(y_p, gain_mix, w_qkv, q_gain, k_gain, seg_ones, head_dim, ROW_TILE,
                                             li, n_c_layers, kv_pr)
            kv_pr = (k_pr, v_pr)
            o_p = _attn_prompt(qb, kb, vb, bias, suffix_ones, n_batch, seq, head_dim)
            y_p = _matmul_residual(y_p, [o_p], [w_out], ROW_TILE)

            qb, k_sa, v_sa, _, _ = _sb_qkv(y_s, gain_mix, w_qkv, q_gain, k_gain, seg_ones, head_dim, ROW_TILE,
                                           li, n_c_layers, kv_sa)
            kv_sa = (k_sa, v_sa)
            k_f, v_f = k_sa[li], v_sa[li]
            head_cols = (jnp.arange(width) // head_dim)[None, :] == jnp.arange(c_heads)[:, None]
            q_s = qb.reshape(nb_s, 1, n_tok, width)
            qbd = jnp.where(head_cols[None, :, None, :], q_s, jnp.zeros_like(q_s)).reshape(
                nb_s, c_heads * n_tok, width)
            bias_rows = jnp.broadcast_to(jnp.repeat(bias, n_tok)[:, None], (c_heads * n_tok, LANES))
            o_s = _attn_sample(page_table, qbd, k_f.reshape(nb_s, n_tok, width), v_f.reshape(nb_s, n_tok, width),
                               cache_k, cache_v, li, bias_rows, page_suffix_ones, head_dim, n_tok)
            o_s = o_s.reshape(n_sample, width)
            y_s = _matmul_residual(y_s, [o_s], [w_out], ROW_TILE)

        gain_mlp = norm_mlp[layer][None, :]
        w_up = mlp_w_up[layer].astype(BF16)
        w_down = mlp_w_down[layer].astype(BF16)
        y_p = _mlp(y_p, gain_mlp, w_up, w_down, ROW_TILE, 1024)
        y_s = _mlp(y_s, gain_mlp, w_up, w_down, ROW_TILE, 1024)

    return (y_p.reshape(n_batch, seq, d_model), y_s.reshape(nb_s, n_tok, d_model),
            k_pr.reshape(n_c_layers, n_batch, seq, c_heads, head_dim),
            v_pr.reshape(n_c_layers, n_batch, seq, c_heads, head_dim),
            k_sa.reshape(n_c_layers, nb_s, n_tok, c_heads, head_dim),
            v_sa.reshape(n_c_layers, nb_s, n_tok, c_heads, head_dim),
            jnp.stack(d_pr), jnp.stack(d_sa), jnp.stack(c_pr), jnp.stack(c_sa), jnp.stack(gv_sa))
```

```python
import functools

import jax
import jax.numpy as jnp
from jax import lax
from jax.experimental import pallas as pl
from jax.experimental.pallas import tpu as pltpu

F32 = jnp.float32
BF16 = jnp.bfloat16
HI = lax.Precision.HIGHEST
NORM_EPS = 1e-6
LOG2E = 1.4426950408889634
LANES = 128
SUBLANES = 8
MXU_TILE = 256
VMEM_LIMIT_BYTES = 48 * 1024 * 1024

GMLP_CHUNK = 128
GDN_CHUNK = 64
GDN_CHUNKS_PER_STEP = 2
CONV_K = 4
ROW_TILE = 512
ATT_BLOCK = 256
ATT_PAIRS_PER_STEP = 1
ATT_TRIPS_PER_BODY = 2
PAGE_BLOCK = 128
SAMPLE_PAGES_PER_STEP = 8
PAD_TOK = 8


def _params(semantics):
    return pltpu.CompilerParams(dimension_semantics=semantics, vmem_limit_bytes=VMEM_LIMIT_BYTES)


def _dot(a, b):
    return jnp.dot(a.astype(BF16), b.astype(BF16), preferred_element_type=F32)


def _dot_nt(a, b):
    return lax.dot_general(a.astype(BF16), b.astype(BF16), (((1,), (1,)), ((), ())),
                           preferred_element_type=F32)


def _dot_tn(a, b):
    return lax.dot_general(a.astype(BF16), b.astype(BF16), (((0,), (0,)), ((), ())),
                           preferred_element_type=F32)


def _dot32(a, b):
    return jnp.dot(a, b, precision=HI, preferred_element_type=F32)


def _dot32_nt(a, b):
    return lax.dot_general(a, b, (((1,), (1,)), ((), ())), precision=HI, preferred_element_type=F32)


def _split_dot(x, ones_mat):
    hi = x.astype(BF16)
    lo = (x - hi.astype(F32)).astype(BF16)
    return (jnp.dot(hi, ones_mat, preferred_element_type=F32)
            + jnp.dot(lo, ones_mat, preferred_element_type=F32))


def _sigmoid(x):
    return 1.0 / (1.0 + jnp.exp(-x))


def _silu(x):
    return x * _sigmoid(x)


def _softplus(x):
    return jnp.maximum(x, 0.0) + jnp.log1p(jnp.exp(-jnp.abs(x)))


def _gelu_tanh(x):
    return x * (0.5 * (1.0 + jnp.tanh(0.7978845608028654 * (x + 0.044715 * (x * x * x)))))


def _rms_rows(x, gain):
    ms = jnp.mean(x * x, axis=-1, keepdims=True)
    return x * lax.rsqrt(ms + NORM_EPS) * gain


def _norm_matmul_kernel(x_ref, g_ref, w_ref, o_ref, hn_ref):
    @pl.when(pl.program_id(1) == 0)
    def _():
        hn_ref[...] = _rms_rows(x_ref[...], g_ref[...]).astype(BF16)

    o_ref[...] = jnp.dot(hn_ref[...], w_ref[...], preferred_element_type=F32)


def _norm_matmul(x, gain, w, tm, tn):
    t, d = x.shape
    n = w.shape[1]
    return pl.pallas_call(
        _norm_matmul_kernel,
        grid=(t // tm, n // tn),
        in_specs=[pl.BlockSpec((tm, d), lambda i, j: (i, 0)),
                  pl.BlockSpec((1, d), lambda i, j: (0, 0)),
                  pl.BlockSpec((d, tn), lambda i, j: (0, j))],
        out_specs=pl.BlockSpec((tm, tn), lambda i, j: (i, j)),
        out_shape=jax.ShapeDtypeStruct((t, n), F32),
        scratch_shapes=[pltpu.VMEM((tm, d), BF16)],
        compiler_params=_params(("parallel", "arbitrary")),
        name="norm_matmul",
    )(x, gain, w)


def _matmul_residual_kernel(n_lhs, *refs):
    res_ref = refs[0]
    a_refs = refs[1:1 + n_lhs]
    w_refs = refs[1 + n_lhs:1 + 2 * n_lhs]
    o_ref = refs[1 + 2 * n_lhs]
    acc = res_ref[...]
    for a_ref, w_ref in zip(a_refs, w_refs):
        acc = acc + jnp.dot(a_ref[...].astype(BF16), w_ref[...], preferred_element_type=F32)
    o_ref[...] = acc


def _matmul_residual(res, lhs_list, w_list, tm):
    t, d = res.shape
    n_lhs = len(lhs_list)
    in_specs = [pl.BlockSpec((tm, d), lambda i: (i, 0))]
    in_specs += [pl.BlockSpec((tm, a.shape[1]), lambda i: (i, 0)) for a in lhs_list]
    in_specs += [pl.BlockSpec(w.shape, lambda i: (0, 0)) for w in w_list]
    return pl.pallas_call(
        functools.partial(_matmul_residual_kernel, n_lhs),
        grid=(t // tm,),
        in_specs=in_specs,
        out_specs=pl.BlockSpec((tm, d), lambda i: (i, 0)),
        out_shape=jax.ShapeDtypeStruct((t, d), F32),
        compiler_params=_params(("parallel",)),
        name="matmul_residual",
    )(res, *lhs_list, *w_list)


def _mlp_kernel(x_ref, g_ref, wu_ref, wd_ref, o_ref, hn_ref):
    f = pl.program_id(1)

    @pl.when(f == 0)
    def _():
        x = x_ref[...]
        hn_ref[...] = _rms_rows(x, g_ref[...]).astype(BF16)
        o_ref[...] = x

    h = jnp.dot(hn_ref[...], wu_ref[...], preferred_element_type=F32)
    h = jnp.maximum(h, 0.0)
    h = (h * h).astype(BF16)
    o_ref[...] += jnp.dot(h, wd_ref[...], preferred_element_type=F32)


def _mlp(x, gain, w_up, w_down, tm, tf):
    t, d = x.shape
    dff = w_up.shape[1]
    return pl.pallas_call(
        _mlp_kernel,
        grid=(t // tm, dff // tf),
        in_specs=[pl.BlockSpec((tm, d), lambda i, f: (i, 0)),
                  pl.BlockSpec((1, d), lambda i, f: (0, 0)),
                  pl.BlockSpec((d, tf), lambda i, f: (0, f)),
                  pl.BlockSpec((tf, d), lambda i, f: (f, 0))],
        out_specs=pl.BlockSpec((tm, d), lambda i, f: (i, 0)),
        out_shape=jax.ShapeDtypeStruct((t, d), F32),
        scratch_shapes=[pltpu.VMEM((tm, d), BF16)],
        compiler_params=_params(("parallel", "arbitrary")),
        name="mlp",
    )(x, gain, w_up, w_down)


def _gmlp_kernel(n_chunks, n_groups, u_ref, v_ref, vn_ref, wmix_ref, bias_ref, oa_ref, vout_ref):
    for c in range(n_chunks):
        rows = slice(c * GMLP_CHUNK, (c + 1) * GMLP_CHUNK)
        for g in range(n_groups):
            cols = slice(g * LANES, (g + 1) * LANES)
            vn = _rms_rows(_gelu_tanh(v_ref[rows, cols]), vn_ref[:, cols])
            vout_ref[rows, cols] = vn
            mixed = jnp.dot(wmix_ref[g], vn.astype(BF16), preferred_element_type=F32) + bias_ref[g]
            oa_ref[rows, cols] = _gelu_tanh(u_ref[rows, cols]) * mixed


def _gmlp(proj, u_blk, v_blk, v_norm, wmix, bias_full, rows_per_step):
    t = proj.shape[0]
    n_groups = wmix.shape[0]
    width = n_groups * LANES
    n_chunks = rows_per_step // GMLP_CHUNK
    return pl.pallas_call(
        functools.partial(_gmlp_kernel, n_chunks, n_groups),
        grid=(t // rows_per_step,),
        in_specs=[pl.BlockSpec((rows_per_step, width), lambda i: (i, u_blk)),
                  pl.BlockSpec((rows_per_step, width), lambda i: (i, v_blk)),
                  pl.BlockSpec((1, width), lambda i: (0, 0)),
                  pl.BlockSpec(wmix.shape, lambda i: (0, 0, 0)),
                  pl.BlockSpec(bias_full.shape, lambda i: (0, 0, 0))],
        out_specs=[pl.BlockSpec((rows_per_step, width), lambda i: (i, 0)),
                   pl.BlockSpec((rows_per_step, width), lambda i: (i, 0))],
        out_shape=[jax.ShapeDtypeStruct((t, width), F32), jax.ShapeDtypeStruct((t, width), F32)],
        compiler_params=_params(("parallel",)),
        name="gmlp",
    )(proj, proj, v_norm, wmix, bias_full)


def _l2norm_rows(x):
    return x * lax.rsqrt(jnp.sum(x * x, axis=-1, keepdims=True) + NORM_EPS)


def _gdn_gates(sm, alog, dtb):
    beta = _sigmoid(sm[:, :LANES])
    g = -jnp.exp(alog) * _softplus(sm[:, LANES:] + dtb)
    return beta, g


def _gdn_out(o, onorm, gate):
    return _rms_rows(o, onorm) * _silu(gate)


def _transpose_rows(x):
    rows = x.shape[0]
    if rows < LANES:
        x = jnp.concatenate([x, jnp.zeros((LANES - rows, LANES), F32)], axis=0)
    return jnp.transpose(x)


def _split3(x):
    hi = x.astype(BF16)
    return hi, (x - hi.astype(F32)).astype(BF16)


def _dot3(a, b, nt=False):
    dims = (((1,), (1,)), ((), ())) if nt else (((1,), (0,)), ((), ()))
    rows = a[0].shape[0]
    both = lax.dot_general(jnp.concatenate([a[0], a[1]], axis=0), b[0], dims, preferred_element_type=F32)
    return both[:rows] + both[rows:] + lax.dot_general(a[0], b[1], dims, preferred_element_type=F32)


def _block_diag(parts, n, mask):
    return tuple(jnp.concatenate([p] * n, axis=0) * mask for p in parts)


def _fold_diag_blocks(full, n, blk):
    r = lax.shift_right_logical(lax.broadcasted_iota(jnp.int32, full.shape, 0), blk.bit_length() - 1)
    c = lax.shift_right_logical(lax.broadcasted_iota(jnp.int32, full.shape, 1), blk.bit_length() - 1)
    kept = jnp.where(r == c, full, 0.0)
    out = kept[0:blk]
    for i in range(1, n):
        out = out + kept[i * blk:(i + 1) * blk]
    return out


def _inv_unit_lower_cat(l_cat, eye_cat, n, mask):
    size = l_cat.shape[0]
    x_bd = _block_diag(_split3(-l_cat), n, mask)
    x = -l_cat
    p = jnp.where(eye_cat, 1.0, 0.0) - l_cat
    m = 2
    while m < size:
        x = _dot3(_split3(x), x_bd)
        x_bd = _block_diag(_split3(x), n, mask)
        p = p + _dot3(_split3(p), x_bd)
        m *= 2
    return p


def _gdn_prompt_kernel(n_batch, n_heads, chunk, x_ref, gate_ref, sm_ref, cw_ref, alog_ref, dtb_ref, onorm_ref,
                       bd_ref, bd2_ref, o_ref, s_out_ref, xbuf_ref, s_ref):
    c_idx = pl.program_id(0)
    hw = n_heads * LANES
    cw = n_heads * chunk
    step_rows = x_ref.shape[1]
    n_chunks = step_rows // chunk

    @pl.when(c_idx == 0)
    def _():
        xbuf_ref[:, 0:SUBLANES, :] = jnp.zeros((n_batch, SUBLANES, xbuf_ref.shape[2]), F32)
        s_ref[...] = jnp.zeros(s_ref.shape, F32)

    r = lax.broadcasted_iota(jnp.int32, (chunk, chunk), 0)
    c = lax.broadcasted_iota(jnp.int32, (chunk, chunk), 1)
    incl_ones = jnp.where(c <= r, 1.0, 0.0)
    row_i = lax.broadcasted_iota(jnp.int32, (chunk, cw), 0)
    lane = lax.broadcasted_iota(jnp.int32, (chunk, cw), 1)
    lane_j = lane & (chunk - 1)
    lane_h = lax.shift_right_logical(lane, chunk.bit_length() - 1)
    incl_cat = lane_j <= row_i
    strict_cat = lane_j < row_i
    eye_cat = lane_j == row_i
    bd_mask = bd_ref[...]
    bd2_mask = bd2_ref[...]

    for b in range(n_batch):
        x = x_ref[b]
        xbuf_ref[b, SUBLANES:SUBLANES + step_rows, :] = x
        conv = cw_ref[CONV_K - 1:CONV_K, :] * x
        for i in range(CONV_K - 1):
            shift = CONV_K - 1 - i
            conv = conv + cw_ref[i:i + 1, :] * xbuf_ref[b, SUBLANES - shift:SUBLANES - shift + step_rows, :]
        xbuf_ref[b, 0:SUBLANES, :] = x[step_rows - SUBLANES:step_rows, :]
        act_all = _silu(conv)
        beta_all, g_all = _gdn_gates(sm_ref[b], alog_ref[...], dtb_ref[...])

        for cc in range(n_chunks):
            rows = slice(cc * chunk, (cc + 1) * chunk)
            act = act_all[rows]
            beta = beta_all[rows]
            g = g_all[rows]
            heads = range(n_heads)
            q = [_l2norm_rows(act[:, h * LANES:(h + 1) * LANES]) * (LANES ** -0.5) for h in heads]
            k = [_l2norm_rows(act[:, hw + h * LANES:hw + (h + 1) * LANES]) for h in heads]
            v = [act[:, 2 * hw + h * LANES:2 * hw + (h + 1) * LANES] for h in heads]
            b_col = [beta[:, h:h + 1] for h in heads]
            kb = [k[h] * b_col[h] for h in heads]

            g_cat = jnp.broadcast_to(g[:, 0:1], (chunk, cw))
            for h in range(1, n_heads):
                g_cat = jnp.where(lane_h == h, g[:, h:h + 1], g_cat)
            gcol_cat = _dot32(incl_ones, g_cat)
            grow_cat = jnp.sum(jnp.where(lane_j >= row_i, g_cat, 0.0), axis=0, keepdims=True)
            decay_cat = jnp.where(incl_cat, jnp.exp(jnp.where(incl_cat, gcol_cat - grow_cat, 0.0)), 0.0)
            gc_col = [gcol_cat[:, h * chunk:h * chunk + 1] for h in heads]
            egc = [jnp.exp(gc_col[h]) for h in heads]
            g_last = [gc_col[h][chunk - 1:chunk] for h in heads]

            k_stack = jnp.concatenate(k, axis=0)
            kk = _fold_diag_blocks(_dot3(_split3(jnp.concatenate(kb, axis=0)), _split3(k_stack), nt=True),
                                   n_heads, chunk)
            qk = _fold_diag_blocks(_dot_nt(jnp.concatenate(q, axis=0), k_stack), n_heads, chunk)
            t_cat = _inv_unit_lower_cat(jnp.where(strict_cat, kk * decay_cat, 0.0), eye_cat, n_heads, bd_mask)
            a_in = (qk * decay_cat).astype(BF16)

            rhs = jnp.concatenate([v[h] * b_col[h] for h in heads] + [kb[h] * egc[h] for h in heads], axis=1)
            uw = jnp.dot(t_cat.astype(BF16), jnp.concatenate([rhs.astype(BF16)] * n_heads, axis=0) * bd2_mask,
                         preferred_element_type=F32)

            s_old = [s_ref[b, h] for h in heads]
            v_new, o_inter = [], []
            for h in heads:
                wq = _dot(jnp.concatenate([uw[:, hw + h * LANES:hw + (h + 1) * LANES], q[h] * egc[h]], axis=0),
                          s_old[h])
                v_new.append(uw[:, h * LANES:(h + 1) * LANES] - wq[:chunk])
                o_inter.append(wq[chunk:])
            vn_cat = jnp.concatenate(v_new, axis=1).astype(BF16)
            o_intra = jnp.dot(a_in, jnp.concatenate([vn_cat] * n_heads, axis=0) * bd2_mask[:, :hw],
                              preferred_element_type=F32)
            for h in heads:
                cols = slice(h * LANES, (h + 1) * LANES)
                k_dec = k[h] * jnp.exp(g_last[h] - gc_col[h])
                v_pad = jnp.concatenate([v_new[h], jnp.zeros((LANES - chunk, LANES), F32)], axis=0)
                s_ref[b, h] = s_old[h] * jnp.exp(g_last[h]) + _dot(_transpose_rows(k_dec), v_pad)
                o_ref[b, rows, cols] = _gdn_out(o_inter[h] + o_intra[:, cols], onorm_ref[...],
                                                gate_ref[b, rows, cols])

    @pl.when(c_idx == pl.num_programs(0) - 1)
    def _():
        s_out_ref[...] = s_ref[...]


def _gdn_prompt(proj, n_batch, seq, n_heads, gate_blk, sm_blk, conv_w, alog, dtb, onorm):
    chunk = GDN_CHUNK
    step = min(GDN_CHUNKS_PER_STEP * chunk, seq)
    assert seq % step == 0 and step % chunk == 0
    qkv_w = 3 * n_heads * LANES
    hw = n_heads * LANES
    cw = n_heads * chunk
    proj3 = proj.reshape(n_batch, seq, proj.shape[1])
    blk_row = (jnp.arange(cw) // chunk)[:, None]
    bd_mask = (blk_row == (jnp.arange(cw) // chunk)[None, :]).astype(BF16)
    bd2_mask = (blk_row == ((jnp.arange(2 * hw) // LANES) % n_heads)[None, :]).astype(BF16)
    const = lambda a: pl.BlockSpec(a.shape, lambda c: (0,) * a.ndim)
    o, s_fin = pl.pallas_call(
        functools.partial(_gdn_prompt_kernel, n_batch, n_heads, chunk),
        grid=(seq // step,),
        in_specs=[pl.BlockSpec((n_batch, step, qkv_w), lambda c: (0, c, 0)),
                  pl.BlockSpec((n_batch, step, hw), lambda c: (0, c, gate_blk)),
                  pl.BlockSpec((n_batch, step, 2 * LANES), lambda c: (0, c, sm_blk)),
                  const(conv_w), const(alog), const(dtb), const(onorm), const(bd_mask), const(bd2_mask)],
        out_specs=[pl.BlockSpec((n_batch, step, hw), lambda c: (0, c, 0)),
                   pl.BlockSpec((n_batch, n_heads, LANES, LANES), lambda c: (0, 0, 0, 0))],
        out_shape=[jax.ShapeDtypeStruct((n_batch, seq, hw), F32),
                   jax.ShapeDtypeStruct((n_batch, n_heads, LANES, LANES), F32)],
        scratch_shapes=[pltpu.VMEM((n_batch, SUBLANES + step, qkv_w), F32),
                        pltpu.VMEM((n_batch, n_heads, LANES, LANES), F32)],
        compiler_params=_params(("arbitrary",)),
        name="gdn_prompt",
    )(proj3, proj3, proj3, conv_w, alog, dtb, onorm, bd_mask, bd2_mask)
    return o.reshape(n_batch * seq, hw), s_fin


def _gdn_sample_kernel(n_heads, n_tok, bb, xp_ref, gate_ref, sm_ref, cw_ref, alog_ref, dtb_ref, onorm_ref,
                       s0_ref, o_ref, s_out_ref):
    hw = n_heads * LANES

    def one_request(b, carry):
        conv = cw_ref[0:1, :] * xp_ref[b, 0:PAD_TOK, :]
        for i in range(1, CONV_K):
            conv = conv + cw_ref[i:i + 1, :] * xp_ref[b, i:i + PAD_TOK, :]
        act = _silu(conv)
        beta, g = _gdn_gates(sm_ref[b], alog_ref[...], dtb_ref[...])
        eg = jnp.exp(g)
        gate = gate_ref[b]
        ks = [_l2norm_rows(act[:, hw + h * LANES:hw + (h + 1) * LANES]) for h in range(n_heads)]
        qs = [_l2norm_rows(act[:, h * LANES:(h + 1) * LANES]) * (LANES ** -0.5) for h in range(n_heads)]
        kq_t = _transpose_rows(jnp.concatenate(ks + qs, axis=0))
        for h in range(n_heads):
            v_h = act[:, 2 * hw + h * LANES:2 * hw + (h + 1) * LANES]
            s = s0_ref[b, h]
            outs = []
            for t in range(n_tok):
                kc = kq_t[:, h * PAD_TOK + t:h * PAD_TOK + t + 1]
                qc = kq_t[:, (n_heads + h) * PAD_TOK + t:(n_heads + h) * PAD_TOK + t + 1]
                e = eg[t:t + 1, h:h + 1]
                ks = jnp.sum(s * kc, axis=0, keepdims=True)
                d = beta[t:t + 1, h:h + 1] * (v_h[t:t + 1, :] - e * ks)
                s = e * s + kc * d
                outs.append(jnp.sum(s * qc, axis=0, keepdims=True))
            outs.append(jnp.zeros((PAD_TOK - n_tok, LANES), F32))
            o = jnp.concatenate(outs, axis=0)
            s_out_ref[b, h] = s
            o_ref[b, :, h * LANES:(h + 1) * LANES] = _gdn_out(
                o, onorm_ref[...], gate[:, h * LANES:(h + 1) * LANES])
        return carry

    lax.fori_loop(0, bb, one_request, 0)


def _gdn_sample(xp, gate, sm, conv_w, alog, dtb, onorm, s0, n_tok, bb):
    nb, n_heads = s0.shape[0], s0.shape[1]
    hw = n_heads * LANES
    qkv_w = 3 * hw
    return pl.pallas_call(
        functools.partial(_gdn_sample_kernel, n_heads, n_tok, bb),
        grid=(nb // bb,),
        in_specs=[pl.BlockSpec((bb, 2 * PAD_TOK, qkv_w), lambda i: (i, 0, 0)),
                  pl.BlockSpec((bb, PAD_TOK, hw), lambda i: (i, 0, 0)),
                  pl.BlockSpec((bb, PAD_TOK, 2 * LANES), lambda i: (i, 0, 0)),
                  pl.BlockSpec((CONV_K, qkv_w), lambda i: (0, 0)),
                  pl.BlockSpec((1, LANES), lambda i: (0, 0)),
                  pl.BlockSpec((1, LANES), lambda i: (0, 0)),
                  pl.BlockSpec((1, LANES), lambda i: (0, 0)),
                  pl.BlockSpec((bb, n_heads, LANES, LANES), lambda i: (i, 0, 0, 0))],
        out_specs=[pl.BlockSpec((bb, PAD_TOK, hw), lambda i: (i, 0, 0)),
                   pl.BlockSpec((bb, n_heads, LANES, LANES), lambda i: (i, 0, 0, 0))],
        out_shape=[jax.ShapeDtypeStruct((nb, PAD_TOK, hw), F32),
                   jax.ShapeDtypeStruct(s0.shape, F32)],
        compiler_params=_params(("parallel",)),
        name="gdn_sample",
    )(xp, gate, sm, conv_w, alog, dtb, onorm, s0)


def _head_rms(r, gain, seg_ones, head_dim):
    ss = jnp.dot((r * r).astype(BF16), seg_ones, preferred_element_type=F32)
    return r * lax.rsqrt(ss * (1.0 / head_dim) + NORM_EPS) * gain


def _sb_qkv_kernel(head_dim, n_prev, transposed, x_ref, g_ref, w_ref, qn_ref, kn_ref, seg_ref, *refs):
    qb_ref, k_ref, v_ref, kb_ref, vb_ref, hn_ref = refs[n_prev:]
    j = pl.program_id(1)

    @pl.when(j == 0)
    def _():
        hn_ref[...] = _rms_rows(x_ref[...], g_ref[...]).astype(BF16)

    r = jnp.dot(hn_ref[...], w_ref[...], preferred_element_type=F32)
    seg = seg_ref[...]
    seg_w = seg.shape[0]
    n_blk = r.shape[1] // seg_w

    @pl.when(j == 0)
    def _():
        for c in range(n_blk):
            cols = slice(c * seg_w, (c + 1) * seg_w)
            qn = _head_rms(r[:, cols], qn_ref[...], seg, head_dim)
            qb_ref[:, cols] = (qn * (head_dim ** -0.5)).astype(BF16)

    @pl.when(j == 1)
    def _():
        for c in range(n_blk):
            cols = slice(c * seg_w, (c + 1) * seg_w)
            kn = _head_rms(r[:, cols], kn_ref[...], seg, head_dim)
            if transposed:
                k_ref[cols, :] = jnp.transpose(kn)
            else:
                k_ref[:, cols] = kn
            kb_ref[:, cols] = kn.astype(BF16)

    @pl.when(j == 2)
    def _():
        if transposed:
            for c in range(n_blk):
                cols = slice(c * seg_w, (c + 1) * seg_w)
                v_ref[cols, :] = jnp.transpose(r[:, cols])
        else:
            v_ref[...] = r
        vb_ref[...] = r.astype(BF16)


def _sb_qkv(x, gain, w, q_gain, k_gain, seg_ones, head_dim, tm, layer, n_layers, kv_stacks, seq=None):
    t, d = x.shape
    width = w.shape[1] // 3
    row_blk = pl.BlockSpec((tm, width), lambda i, j: (i, 0))
    if seq is None:
        stack_shape = (n_layers, t, width)
        stack_blk = pl.BlockSpec((None, tm, width), lambda i, j: (layer, i, 0))
    else:
        per_seq = seq // tm
        stack_shape = (n_layers, t // seq, width, seq)
        stack_blk = pl.BlockSpec((None, None, width, tm), lambda i, j: (layer, i // per_seq, 0, i % per_seq))
    n_fixed = 6
    return pl.pallas_call(
        functools.partial(_sb_qkv_kernel, head_dim, len(kv_stacks), seq is not None),
        grid=(t // tm, 3),
        in_specs=[pl.BlockSpec((tm, d), lambda i, j: (i, 0)),
                  pl.BlockSpec((1, d), lambda i, j: (0, 0)),
                  pl.BlockSpec((d, width), lambda i, j: (0, j)),
                  pl.BlockSpec(q_gain.shape, lambda i, j: (0, 0)),
                  pl.BlockSpec(k_gain.shape, lambda i, j: (0, 0)),
                  pl.BlockSpec(seg_ones.shape, lambda i, j: (0, 0))]
        + [pl.BlockSpec(memory_space=pl.ANY) for _ in kv_stacks],
        out_specs=[row_blk, stack_blk, stack_blk, row_blk, row_blk],
        out_shape=[jax.ShapeDtypeStruct((t, width), BF16),
                   jax.ShapeDtypeStruct(stack_shape, F32),
                   jax.ShapeDtypeStruct(stack_shape, F32),
                   jax.ShapeDtypeStruct((t, width), BF16),
                   jax.ShapeDtypeStruct((t, width), BF16)],
        input_output_aliases={n_fixed + s: 1 + s for s in range(len(kv_stacks))},
        scratch_shapes=[pltpu.VMEM((tm, d), BF16)],
        compiler_params=_params(("parallel", "arbitrary")),
        name="sb_qkv",
    )(x, gain, w, q_gain, k_gain, seg_ones, *kv_stacks)


def _attn_prompt_kernel(head_dim, bias_ref, q_ref, k_ref, v_ref, suf_ref, o_ref,
                        acc_ref, carry_ref, z_ref, cs_ref, rs_ref):
    g = pl.program_id(1)
    qi = pl.program_id(2)
    tq = q_ref.shape[0]
    tk = suf_ref.shape[0]
    n_pairs = q_ref.shape[1] // LANES
    rows = 2 * tq
    lane = lax.broadcasted_iota(jnp.int32, (tq, LANES), 1)
    first = lane < head_dim
    row1 = lax.broadcasted_iota(jnp.int32, (rows, 1), 0)
    qq, bias = [], []
    for pp in range(n_pairs):
        q = q_ref[:, pp * LANES:(pp + 1) * LANES].astype(F32)
        qq.append(jnp.concatenate([jnp.where(first, q, 0.0), jnp.where(first, 0.0, q)], axis=0).astype(BF16))
        head0 = 2 * (g * n_pairs + pp)
        bias.append(jnp.where(row1 < tq, bias_ref[head0], bias_ref[head0 + 1]))
    acc_ref[...] = jnp.zeros(acc_ref.shape, F32)
    carry_ref[...] = jnp.zeros(carry_ref.shape, F32)

    def causal_mask():
        r2 = lax.broadcasted_iota(jnp.int32, (rows, tk), 0)
        c2 = lax.broadcasted_iota(jnp.int32, (rows, tk), 1)
        return c2 < jnp.where(r2 >= tq, r2 - tq, r2)

    def logits(j, zs):
        start = pl.multiple_of(j * tk, tk)
        for pp in range(n_pairs):
            kblk = k_ref[pl.ds(start, tk), pp * LANES:(pp + 1) * LANES]
            z_ref[pp, zs] = lax.dot_general(qq[pp], kblk, (((1,), (1,)), ((), ())),
                                            preferred_element_type=F32) + bias[pp]

    def keeps(zs, cs_slot, masked):
        for pp in range(n_pairs):
            z = z_ref[pp, zs]
            sp = jnp.maximum(z, 0.0) + jnp.log(1.0 + jnp.exp2(jnp.abs(z) * (-LOG2E)))
            if masked:
                sp = jnp.where(causal_mask(), sp, 0.0)
            cs_ref[pp, cs_slot] = jnp.dot(sp.astype(BF16), suf_ref[...], preferred_element_type=F32)
            rs_ref[pp, cs_slot] = jnp.broadcast_to(-jnp.sum(sp, axis=1, keepdims=True), (rows, LANES))

    def weigh(j, zs, cs_slot, masked, live=None):
        start = pl.multiple_of(j * tk, tk)
        for pp in range(n_pairs):
            vblk = v_ref[pl.ds(start, tk), pp * LANES:(pp + 1) * LANES]
            carry = carry_ref[pp]
            a = jnp.exp(z_ref[pp, zs] + cs_ref[pp, cs_slot] + jnp.concatenate([carry] * (tk // LANES), axis=1))
            if masked:
                a = jnp.where(causal_mask(), a, 0.0)
            rs = rs_ref[pp, cs_slot]
            if live is not None:
                a = jnp.where(live, a, 0.0)
                rs = jnp.where(live, rs, 0.0)
            acc_ref[pp] += jnp.dot(a.astype(BF16), vblk, preferred_element_type=F32)
            carry_ref[pp] = carry + rs

    def next3(s):
        return jnp.where(s == 2, 0, s + 1)

    logits(qi, 2)
    logits(jnp.maximum(qi - 1, 0), 0)
    keeps(2, 1, True)
    logits(jnp.maximum(qi - 2, 0), 1)
    weigh(qi, 2, 1, True)
    keeps(0, 0, False)

    def trip(j, zc, cc):
        zb = next3(zc)
        weigh(j, zc, cc, False)
        keeps(zb, 1 - cc, False)
        logits(j - 2, next3(zb))
        return zb, 1 - cc

    def unrolled_trips(t, slots):
        for u in range(ATT_TRIPS_PER_BODY):
            slots = trip(qi - 1 - ATT_TRIPS_PER_BODY * t - u, *slots)
        return slots

    n_trips = jnp.maximum(qi - 2, 0)
    n_bodies = lax.shift_right_logical(n_trips, ATT_TRIPS_PER_BODY.bit_length() - 1)
    slots = lax.fori_loop(0, n_bodies, unrolled_trips, (0, 0))
    first_left = qi - 1 - ATT_TRIPS_PER_BODY * n_bodies
    zc, cc = lax.fori_loop(0, n_trips & (ATT_TRIPS_PER_BODY - 1),
                           lambda r, s: trip(first_left - r, *s), slots)

    zb = next3(zc)
    weigh(jnp.minimum(qi, 1), zc, cc, False, live=qi >= 2)
    keeps(zb, 1 - cc, False)
    weigh(0, zb, 1 - cc, False, live=qi >= 1)

    for pp in range(n_pairs):
        acc = acc_ref[pp]
        o_ref[:, pp * LANES:(pp + 1) * LANES] = jnp.where(first, acc[:tq], acc[tq:])


def _attn_prompt(qb, kb, vb, bias, suffix_ones, n_batch, seq, head_dim):
    width = qb.shape[1]
    tq = suffix_ones.shape[0]
    nq = seq // tq
    npp = ATT_PAIRS_PER_STEP
    blk_w = npp * LANES
    assert LANES == 2 * head_dim and seq % tq == 0 and width % blk_w == 0
    return pl.pallas_call(
        functools.partial(_attn_prompt_kernel, head_dim),
        grid=(n_batch, width // blk_w, nq),
        in_specs=[pl.BlockSpec(memory_space=pltpu.SMEM),
                  pl.BlockSpec((tq, blk_w), lambda b, p, i: (b * nq + i, p)),
                  pl.BlockSpec((seq, blk_w), lambda b, p, i: (b, p)),
                  pl.BlockSpec((seq, blk_w), lambda b, p, i: (b, p)),
                  pl.BlockSpec(suffix_ones.shape, lambda b, p, i: (0, 0))],
        out_specs=pl.BlockSpec((tq, blk_w), lambda b, p, i: (b * nq + i, p)),
        out_shape=jax.ShapeDtypeStruct((n_batch * seq, width), F32),
        scratch_shapes=[pltpu.VMEM((npp, 2 * tq, LANES), F32), pltpu.VMEM((npp, 2 * tq, LANES), F32),
                        pltpu.VMEM((npp, 3, 2 * tq, tq), F32), pltpu.VMEM((npp, 2, 2 * tq, tq), F32),
                        pltpu.VMEM((npp, 2, 2 * tq, LANES), F32)],
        compiler_params=_params(("parallel", "parallel", "arbitrary")),
        name="attn_prompt",
    )(bias, qb, kb, vb, suffix_ones)


def _attn_sample_kernel(head_dim, n_tok, n_kv, pt_ref, qbd_ref, kn_ref, vn_ref, *refs):
    kv_refs = refs[:n_kv]
    bias_ref, suf_ref, o_ref, acc_ref, carry_ref = refs[n_kv:]
    j = pl.program_id(1)
    n_rows, width = acc_ref.shape
    row_head = lax.shift_right_logical(lax.broadcasted_iota(jnp.int32, (n_rows, width), 0),
                                       n_tok.bit_length() - 1)
    col_head = lax.shift_right_logical(lax.broadcasted_iota(jnp.int32, (n_rows, width), 1),
                                       head_dim.bit_length() - 1)
    own = row_head == col_head
    bias = bias_ref[...]
    qbd = qbd_ref[0]

    @pl.when(j == 0)
    def _():
        qf = qbd.astype(F32)
        tok = lax.broadcasted_iota(jnp.int32, (n_rows, 1), 0) & (n_tok - 1)
        bcol = bias[:, 0:1]
        lks, lss, valids = [], [], []
        for s in range(n_tok):
            z = jnp.sum(qf * kn_ref[0, s:s + 1, :], axis=-1, keepdims=True) + bcol
            sp = _softplus(z)
            valid = tok > s
            valids.append(valid)
            lks.append(jnp.where(valid, -sp, 0.0))
            lss.append(z - sp)
        later = jnp.zeros((n_rows, 1), F32)
        acc = jnp.zeros((n_rows, width), F32)
        for s in reversed(range(n_tok)):
            a = jnp.where(valids[s], jnp.exp(lss[s] + later), 0.0)
            acc = acc + a * vn_ref[0, s:s + 1, :]
            later = later + lks[s]
        acc_ref[...] = acc
        carry_ref[...] = jnp.broadcast_to(later, carry_ref.shape)

    k_refs, v_refs = kv_refs[0::2], kv_refs[1::2]
    zs = [jnp.dot(qbd, k_ref[...].astype(BF16), preferred_element_type=F32) + bias for k_ref in k_refs]
    sps = [jnp.maximum(z, 0.0) + jnp.log(1.0 + jnp.exp2(jnp.abs(z) * (-LOG2E))) for z in zs]
    css = [jnp.dot(sp.astype(BF16), suf_ref[...], preferred_element_type=F32) for sp in sps]
    carry = carry_ref[...]
    acc = acc_ref[...]
    for z, sp, cs, v_ref in zip(zs, sps, css, v_refs):
        a = jnp.exp(z + cs + carry)
        acc = acc + lax.dot_general(a.astype(BF16), v_ref[...].astype(BF16), (((1,), (1,)), ((), ())),
                                    preferred_element_type=F32)
        carry = carry - jnp.sum(sp, axis=1, keepdims=True)
    carry_ref[...] = carry
    acc_ref[...] = acc

    @pl.when(j == pl.num_programs(1) - 1)
    def _():
        own_acc = jnp.where(own, acc_ref[...], 0.0)
        sel_r = lax.broadcasted_iota(jnp.int32, (SUBLANES, n_rows), 0)
        sel_c = lax.broadcasted_iota(jnp.int32, (SUBLANES, n_rows), 1) & (n_tok - 1)
        o = _dot32(jnp.where(sel_r == sel_c, 1.0, 0.0), own_acc)
        o_ref[0] = o[:n_tok]


def _attn_sample(page_table, qbd, k_new, v_new, cache_k, cache_v, layer, bias_rows, suffix_ones,
                 head_dim, n_tok):
    nb, n_pages = page_table.shape
    n_layers, n_pool, page, n_heads, _ = cache_k.shape
    width = n_heads * head_dim
    ck = jnp.transpose(cache_k, (0, 1, 3, 4, 2)).reshape(n_layers * n_pool, width, page)
    cv = jnp.transpose(cache_v, (0, 1, 3, 4, 2)).reshape(n_layers * n_pool, width, page)
    n_rows = n_heads * n_tok
    base = layer * n_pool
    per_step = min(SAMPLE_PAGES_PER_STEP, n_pages)
    assert n_pages % per_step == 0 and n_tok & (n_tok - 1) == 0 and n_tok <= SUBLANES

    def page_spec(g):
        def page_idx(b, j, pt):
            return (base + pt[b * n_pages + (n_pages - 1 - (j * per_step + g))], 0, 0)
        return pl.BlockSpec((None, width, page), page_idx)

    kv_specs, kv_args = [], []
    for g in range(per_step):
        kv_specs += [page_spec(g), page_spec(g)]
        kv_args += [ck, cv]
    tok_blk = pl.BlockSpec((1, n_tok, width), lambda b, j, pt: (b, 0, 0))
    grid_spec = pltpu.PrefetchScalarGridSpec(
        num_scalar_prefetch=1,
        grid=(nb, n_pages // per_step),
        in_specs=[pl.BlockSpec((1, n_rows, width), lambda b, j, pt: (b, 0, 0)), tok_blk, tok_blk] + kv_specs + [
            pl.BlockSpec((n_rows, LANES), lambda b, j, pt: (0, 0)),
            pl.BlockSpec(suffix_ones.shape, lambda b, j, pt: (0, 0))],
        out_specs=tok_blk,
        scratch_shapes=[pltpu.VMEM((n_rows, width), F32),
                        pltpu.VMEM((n_rows, LANES), F32)],
    )
    return pl.pallas_call(
        functools.partial(_attn_sample_kernel, head_dim, n_tok, 2 * per_step),
        grid_spec=grid_spec,
        out_shape=jax.ShapeDtypeStruct((nb, n_tok, width), F32),
        compiler_params=_params(("parallel", "arbitrary")),
        name="attn_sample",
    )(page_table.reshape(-1), qbd, k_new, v_new, *kv_args, bias_rows, suffix_ones)


def _pad_lanes(x, width):
    return jnp.pad(x, [(0, 0)] * (x.ndim - 1) + [(0, width - x.shape[-1])])


def _pad_tokens(x, total):
    return jnp.pad(x, [(0, 0), (0, total - x.shape[1]), (0, 0)])


def kernel(x_prompt, x_sample, cache_k, cache_v, state_delta, state_conv, page_table, norm_mix, norm_mlp,
           ab_w_in, ab_w_out, gmlp_w_s, gmlp_b_s, gmlp_v_norm, gdn_conv_w, gdn_a_log, gdn_dt_bias,
           gdn_out_norm, sb_w_qkv, sb_w_out, sb_q_norm, sb_k_norm, sb_logit_bias, mlp_w_up, mlp_w_down):
    n_batch, seq, d_model = x_prompt.shape
    nb_s, n_tok, _ = x_sample.shape
    depth = norm_mix.shape[0]
    n_groups = gmlp_w_s.shape[1]
    a_width = n_groups * LANES
    n_bheads = state_delta.shape[2]
    b_qkv = state_conv.shape[-1]
    b_width = n_bheads * LANES
    c_heads, head_dim = cache_k.shape[3], cache_k.shape[4]
    n_prompt = n_batch * seq
    n_sample = nb_s * n_tok
    assert a_width == b_width and b_qkv == 3 * b_width and LANES % head_dim == 0
    assert seq % ROW_TILE == 0 and n_sample % ROW_TILE == 0 and n_sample % GMLP_CHUNK == 0
    assert GMLP_CHUNK % n_tok == 0 and n_tok <= PAD_TOK

    y_p = x_prompt.reshape(n_prompt, d_model)
    y_s = x_sample.reshape(n_sample, d_model)
    assert cache_k.shape[2] == PAGE_BLOCK
    pos = jnp.arange(PAGE_BLOCK)
    page_suffix_ones = -(pos[:, None] >= pos[None, :]).astype(BF16)
    blk = jnp.arange(min(ATT_BLOCK, seq))
    suffix_ones = -(blk[:, None] >= blk[None, :]).astype(BF16)
    seg = jnp.arange(MXU_TILE) // head_dim
    seg_ones = (seg[:, None] == seg[None, :]).astype(BF16)

    n_c_layers = depth // 2
    kv_pr, kv_sa = (), ()
    d_pr, d_sa, c_pr, c_sa, gv_sa = [], [], [], [], []
    for layer in range(depth):
        li = layer // 2
        gain_mix = norm_mix[layer][None, :]
        if layer % 2 == 0:
            w_in = ab_w_in[li]
            o = 0
            a_u, o = w_in[:, o:o + a_width], o + a_width
            a_v, o = w_in[:, o:o + a_width], o + a_width
            qkv_w, o = w_in[:, o:o + b_qkv], o + b_qkv
            beta_w, o = w_in[:, o:o + n_bheads], o + n_bheads
            araw_w, o = w_in[:, o:o + n_bheads], o + n_bheads
            gate_w = w_in[:, o:]
            w_cat = jnp.concatenate([qkv_w, a_u, a_v, gate_w, _pad_lanes(beta_w, LANES),
                                     _pad_lanes(araw_w, LANES)], axis=1).astype(BF16)
            u_blk = b_qkv // a_width
            v_blk = u_blk + 1
            gate_blk = u_blk + 2
            sm_blk = (b_qkv + 3 * a_width) // (2 * LANES)
            tn = w_cat.shape[1] // 2
            w_out = ab_w_out[li].astype(BF16)
            alog = _pad_lanes(gdn_a_log[li][None, :], LANES)
            dtb = _pad_lanes(gdn_dt_bias[li][None, :], LANES)
            onorm = gdn_out_norm[li][None, :]
            conv_w = gdn_conv_w[li]
            v_norm = gmlp_v_norm[li][None, :]
            tril = jnp.tril(jnp.ones((GMLP_CHUNK, GMLP_CHUNK), bool))
            wmix_p = jnp.where(tril, gmlp_w_s[li], 0.0).astype(BF16)
            bias_p = jnp.broadcast_to(gmlp_b_s[li][:, :, None], (n_groups, GMLP_CHUNK, LANES))
            small = jnp.where(tril[:n_tok, :n_tok], gmlp_w_s[li][:, :n_tok, :n_tok], 0.0)
            eye_req = jnp.eye(GMLP_CHUNK // n_tok, dtype=F32)
            wmix_s = jnp.einsum('ab,gij->gaibj', eye_req, small).reshape(
                n_groups, GMLP_CHUNK, GMLP_CHUNK).astype(BF16)
            bias_s = jnp.broadcast_to(
                jnp.tile(gmlp_b_s[li][:, :n_tok], (1, GMLP_CHUNK // n_tok))[:, :, None],
                (n_groups, GMLP_CHUNK, LANES))

            proj_p = _norm_matmul(y_p, gain_mix, w_cat, ROW_TILE, tn)
            oa_p, _ = _gmlp(proj_p, u_blk, v_blk, v_norm, wmix_p, bias_p, ROW_TILE)
            ob_p, s_p = _gdn_prompt(proj_p, n_batch, seq, n_bheads, gate_blk, sm_blk, conv_w, alog, dtb, onorm)
            c_pr.append(proj_p[:, :b_qkv].reshape(n_batch, seq, b_qkv)[:, seq - (CONV_K - 1):])
            d_pr.append(s_p)
            y_p = _matmul_residual(y_p, [oa_p, ob_p], [w_out[:a_width], w_out[a_width:]], ROW_TILE)

            proj_s = _norm_matmul(y_s, gain_mix, w_cat, ROW_TILE, tn)
            oa_s, vn_s = _gmlp(proj_s, u_blk, v_blk, v_norm, wmix_s, bias_s, ROW_TILE)
            proj_s3 = proj_s.reshape(nb_s, n_tok, -1)
            qkv_s = proj_s3[:, :, :b_qkv]
            xp = _pad_tokens(jnp.concatenate([state_conv[li], qkv_s], axis=1), 2 * PAD_TOK)
            gate_s = _pad_tokens(proj_s3[:, :, gate_blk * a_width:(gate_blk + 1) * a_width], PAD_TOK)
            sm_s = _pad_tokens(proj_s3[:, :, sm_blk * 2 * LANES:], PAD_TOK)
            ob_s, s_s = _gdn_sample(xp, gate_s, sm_s, conv_w, alog, dtb, onorm, state_delta[li], n_tok, 8)
            ob_s = ob_s[:, :n_tok].reshape(n_sample, b_width)
            c_sa.append(qkv_s[:, n_tok - (CONV_K - 1):])
            d_sa.append(s_s)
            gv_sa.append(vn_s.reshape(nb_s, n_tok, a_width))
            y_s = _matmul_residual(y_s, [oa_s, ob_s], [w_out[:a_width], w_out[a_width:]], ROW_TILE)
        else:
            w_qkv = sb_w_qkv[li].astype(BF16)
            w_out = sb_w_out[li].astype(BF16)
            width = c_heads * head_dim
            q_gain = jnp.tile(sb_q_norm[li], MXU_TILE // head_dim)[None, :]
            k_gain = jnp.tile(sb_k_norm[li], MXU_TILE // head_dim)[None, :]
            bias = sb_logit_bias[li]

            qb, k_pr, v_pr, kb, vb = _sb_qkv(y_p, gain_mix, w_qkv, q_gain, k_gain, seg_ones, head_dim, ROW_TILE,
                                             li, n_c_layers, kv_pr, seq=seq)
            kv_pr = (k_pr, v_pr)
            o_p = _attn_prompt(qb, kb, vb, bias, suffix_ones, n_batch, seq, head_dim)
            y_p = _matmul_residual(y_p, [o_p], [w_out], ROW_TILE)

            qb, k_sa, v_sa, _, _ = _sb_qkv(y_s, gain_mix, w_qkv, q_gain, k_gain, seg_ones, head_dim, ROW_TILE,
                                           li, n_c_layers, kv_sa)
            kv_sa = (k_sa, v_sa)
            k_f, v_f = k_sa[li], v_sa[li]
            head_cols = (jnp.arange(width) // head_dim)[None, :] == jnp.arange(c_heads)[:, None]
            q_s = qb.reshape(nb_s, 1, n_tok, width)
            qbd = jnp.where(head_cols[None, :, None, :], q_s, jnp.zeros_like(q_s)).reshape(
                nb_s, c_heads * n_tok, width)
            bias_rows = jnp.broadcast_to(jnp.repeat(bias, n_tok)[:, None], (c_heads * n_tok, LANES))
            o_s = _attn_sample(page_table, qbd, k_f.reshape(nb_s, n_tok, width), v_f.reshape(nb_s, n_tok, width),
                               cache_k, cache_v, li, bias_rows, page_suffix_ones, head_dim, n_tok)
            o_s = o_s.reshape(n_sample, width)
            y_s = _matmul_residual(y_s, [o_s], [w_out], ROW_TILE)

        gain_mlp = norm_mlp[layer][None, :]
        w_up = mlp_w_up[layer].astype(BF16)
        w_down = mlp_w_down[layer].astype(BF16)
        y_p = _mlp(y_p, gain_mlp, w_up, w_down, ROW_TILE, 1024)
        y_s = _mlp(y_s, gain_mlp, w_up, w_down, ROW_TILE, 1024)

    return (y_p.reshape(n_batch, seq, d_model), y_s.reshape(nb_s, n_tok, d_model),
            k_pr.reshape(n_c_layers, n_batch, c_heads, head_dim, seq).transpose(0, 1, 4, 2, 3),
            v_pr.reshape(n_c_layers, n_batch, c_heads, head_dim, seq).transpose(0, 1, 4, 2, 3),
            k_sa.reshape(n_c_layers, nb_s, n_tok, c_heads, head_dim),
            v_sa.reshape(n_c_layers, nb_s, n_tok, c_heads, head_dim),
            jnp.stack(d_pr), jnp.stack(d_sa), jnp.stack(c_pr), jnp.stack(c_sa), jnp.stack(gv_sa))
```

```python
import functools

import jax
import jax.numpy as jnp
from jax import lax
from jax.experimental import pallas as pl
from jax.experimental.pallas import tpu as pltpu

F32 = jnp.float32
BF16 = jnp.bfloat16
HI = lax.Precision.HIGHEST
NORM_EPS = 1e-6
LOG2E = 1.4426950408889634
LANES = 128
SUBLANES = 8
MXU_TILE = 256
VMEM_LIMIT_BYTES = 48 * 1024 * 1024

GMLP_CHUNK = 128
GDN_CHUNK = 64
GDN_CHUNKS_PER_STEP = 2
CONV_K = 4
ROW_TILE = 512
MLP_ROW_TILE = 1024
ATT_BLOCK = 256
ATT_PAIRS_PER_STEP = 1
ATT_TRIPS_PER_BODY = 2
PAGE_BLOCK = 128
SAMPLE_PAGES_PER_STEP = 8
PAD_TOK = 8


def _params(semantics):
    return pltpu.CompilerParams(dimension_semantics=semantics, vmem_limit_bytes=VMEM_LIMIT_BYTES)


def _dot(a, b):
    return jnp.dot(a.astype(BF16), b.astype(BF16), preferred_element_type=F32)


def _dot_nt(a, b):
    return lax.dot_general(a.astype(BF16), b.astype(BF16), (((1,), (1,)), ((), ())),
                           preferred_element_type=F32)


def _dot_tn(a, b):
    return lax.dot_general(a.astype(BF16), b.astype(BF16), (((0,), (0,)), ((), ())),
                           preferred_element_type=F32)


def _dot32(a, b):
    return jnp.dot(a, b, precision=HI, preferred_element_type=F32)


def _dot32_nt(a, b):
    return lax.dot_general(a, b, (((1,), (1,)), ((), ())), precision=HI, preferred_element_type=F32)


def _split_dot(x, ones_mat):
    hi = x.astype(BF16)
    lo = (x - hi.astype(F32)).astype(BF16)
    return (jnp.dot(hi, ones_mat, preferred_element_type=F32)
            + jnp.dot(lo, ones_mat, preferred_element_type=F32))


def _sigmoid(x):
    return 1.0 / (1.0 + jnp.exp(-x))


def _silu(x):
    return x * _sigmoid(x)


def _softplus(x):
    return jnp.maximum(x, 0.0) + jnp.log1p(jnp.exp(-jnp.abs(x)))


def _gelu_tanh(x):
    return x * (0.5 * (1.0 + jnp.tanh(0.7978845608028654 * (x + 0.044715 * (x * x * x)))))


def _rms_rows(x, gain):
    ms = jnp.mean(x * x, axis=-1, keepdims=True)
    return x * lax.rsqrt(ms + NORM_EPS) * gain


def _norm_matmul_kernel(x_ref, g_ref, w_ref, o_ref, hn_ref):
    @pl.when(pl.program_id(1) == 0)
    def _():
        hn_ref[...] = _rms_rows(x_ref[...], g_ref[...]).astype(BF16)

    o_ref[...] = jnp.dot(hn_ref[...], w_ref[...], preferred_element_type=F32)


def _norm_matmul(x, gain, w, tm, tn):
    t, d = x.shape
    n = w.shape[1]
    return pl.pallas_call(
        _norm_matmul_kernel,
        grid=(t // tm, n // tn),
        in_specs=[pl.BlockSpec((tm, d), lambda i, j: (i, 0)),
                  pl.BlockSpec((1, d), lambda i, j: (0, 0)),
                  pl.BlockSpec((d, tn), lambda i, j: (0, j))],
        out_specs=pl.BlockSpec((tm, tn), lambda i, j: (i, j)),
        out_shape=jax.ShapeDtypeStruct((t, n), F32),
        scratch_shapes=[pltpu.VMEM((tm, d), BF16)],
        compiler_params=_params(("parallel", "arbitrary")),
        name="norm_matmul",
    )(x, gain, w)


def _matmul_residual_kernel(n_lhs, *refs):
    res_ref = refs[0]
    a_refs = refs[1:1 + n_lhs]
    w_refs = refs[1 + n_lhs:1 + 2 * n_lhs]
    o_ref = refs[1 + 2 * n_lhs]
    acc = res_ref[...]
    for a_ref, w_ref in zip(a_refs, w_refs):
        acc = acc + jnp.dot(a_ref[...].astype(BF16), w_ref[...], preferred_element_type=F32)
    o_ref[...] = acc


def _matmul_residual(res, lhs_list, w_list, tm):
    t, d = res.shape
    n_lhs = len(lhs_list)
    in_specs = [pl.BlockSpec((tm, d), lambda i: (i, 0))]
    in_specs += [pl.BlockSpec((tm, a.shape[1]), lambda i: (i, 0)) for a in lhs_list]
    in_specs += [pl.BlockSpec(w.shape, lambda i: (0, 0)) for w in w_list]
    return pl.pallas_call(
        functools.partial(_matmul_residual_kernel, n_lhs),
        grid=(t // tm,),
        in_specs=in_specs,
        out_specs=pl.BlockSpec((tm, d), lambda i: (i, 0)),
        out_shape=jax.ShapeDtypeStruct((t, d), F32),
        compiler_params=_params(("parallel",)),
        name="matmul_residual",
    )(res, *lhs_list, *w_list)


def _mlp_kernel(x_ref, g_ref, wu_ref, wd_ref, o_ref, hn_ref):
    f = pl.program_id(1)

    @pl.when(f == 0)
    def _():
        x = x_ref[...]
        hn_ref[...] = _rms_rows(x, g_ref[...]).astype(BF16)
        o_ref[...] = x

    h = jnp.dot(hn_ref[...], wu_ref[...], preferred_element_type=F32)
    h = jnp.maximum(h, 0.0)
    h = (h * h).astype(BF16)
    o_ref[...] += jnp.dot(h, wd_ref[...], preferred_element_type=F32)


def _mlp(x, gain, w_up, w_down, tm, tf):
    t, d = x.shape
    dff = w_up.shape[1]
    return pl.pallas_call(
        _mlp_kernel,
        grid=(t // tm, dff // tf),
        in_specs=[pl.BlockSpec((tm, d), lambda i, f: (i, 0)),
                  pl.BlockSpec((1, d), lambda i, f: (0, 0)),
                  pl.BlockSpec((d, tf), lambda i, f: (0, f)),
                  pl.BlockSpec((tf, d), lambda i, f: (f, 0))],
        out_specs=pl.BlockSpec((tm, d), lambda i, f: (i, 0)),
        out_shape=jax.ShapeDtypeStruct((t, d), F32),
        scratch_shapes=[pltpu.VMEM((tm, d), BF16)],
        compiler_params=_params(("parallel", "arbitrary")),
        name="mlp",
    )(x, gain, w_up, w_down)


def _gmlp_kernel(n_chunks, n_groups, u_ref, v_ref, vn_ref, wmix_ref, bias_ref, oa_ref, vout_ref):
    for c in range(n_chunks):
        rows = slice(c * GMLP_CHUNK, (c + 1) * GMLP_CHUNK)
        for g in range(n_groups):
            cols = slice(g * LANES, (g + 1) * LANES)
            vn = _rms_rows(_gelu_tanh(v_ref[rows, cols]), vn_ref[:, cols])
            vout_ref[rows, cols] = vn
            mixed = jnp.dot(wmix_ref[g], vn.astype(BF16), preferred_element_type=F32) + bias_ref[g]
            oa_ref[rows, cols] = _gelu_tanh(u_ref[rows, cols]) * mixed


def _gmlp(proj, u_blk, v_blk, v_norm, wmix, bias_full, rows_per_step):
    t = proj.shape[0]
    n_groups = wmix.shape[0]
    width = n_groups * LANES
    n_chunks = rows_per_step // GMLP_CHUNK
    return pl.pallas_call(
        functools.partial(_gmlp_kernel, n_chunks, n_groups),
        grid=(t // rows_per_step,),
        in_specs=[pl.BlockSpec((rows_per_step, width), lambda i: (i, u_blk)),
                  pl.BlockSpec((rows_per_step, width), lambda i: (i, v_blk)),
                  pl.BlockSpec((1, width), lambda i: (0, 0)),
                  pl.BlockSpec(wmix.shape, lambda i: (0, 0, 0)),
                  pl.BlockSpec(bias_full.shape, lambda i: (0, 0, 0))],
        out_specs=[pl.BlockSpec((rows_per_step, width), lambda i: (i, 0)),
                   pl.BlockSpec((rows_per_step, width), lambda i: (i, 0))],
        out_shape=[jax.ShapeDtypeStruct((t, width), F32), jax.ShapeDtypeStruct((t, width), F32)],
        compiler_params=_params(("parallel",)),
        name="gmlp",
    )(proj, proj, v_norm, wmix, bias_full)


def _l2norm_rows(x):
    return x * lax.rsqrt(jnp.sum(x * x, axis=-1, keepdims=True) + NORM_EPS)


def _gdn_gates(sm, alog, dtb):
    beta = _sigmoid(sm[:, :LANES])
    g = -jnp.exp(alog) * _softplus(sm[:, LANES:] + dtb)
    return beta, g


def _gdn_out(o, onorm, gate):
    return _rms_rows(o, onorm) * _silu(gate)


def _transpose_rows(x):
    rows = x.shape[0]
    if rows < LANES:
        x = jnp.concatenate([x, jnp.zeros((LANES - rows, LANES), F32)], axis=0)
    return jnp.transpose(x)


def _split3(x):
    hi = x.astype(BF16)
    return hi, (x - hi.astype(F32)).astype(BF16)


def _dot3(a, b, nt=False):
    dims = (((1,), (1,)), ((), ())) if nt else (((1,), (0,)), ((), ()))
    rows = a[0].shape[0]
    both = lax.dot_general(jnp.concatenate([a[0], a[1]], axis=0), b[0], dims, preferred_element_type=F32)
    return both[:rows] + both[rows:] + lax.dot_general(a[0], b[1], dims, preferred_element_type=F32)


def _block_diag(parts, n, mask):
    return tuple(jnp.concatenate([p] * n, axis=0) * mask for p in parts)


def _fold_diag_blocks(full, n, blk):
    r = lax.shift_right_logical(lax.broadcasted_iota(jnp.int32, full.shape, 0), blk.bit_length() - 1)
    c = lax.shift_right_logical(lax.broadcasted_iota(jnp.int32, full.shape, 1), blk.bit_length() - 1)
    kept = jnp.where(r == c, full, 0.0)
    out = kept[0:blk]
    for i in range(1, n):
        out = out + kept[i * blk:(i + 1) * blk]
    return out


def _inv_unit_lower_cat(l_cat, eye_cat, n, mask):
    size = l_cat.shape[0]
    x_bd = _block_diag(_split3(-l_cat), n, mask)
    x = -l_cat
    p = jnp.where(eye_cat, 1.0, 0.0) - l_cat
    m = 2
    while m < size:
        x = _dot3(_split3(x), x_bd)
        x_bd = _block_diag(_split3(x), n, mask)
        p = p + _dot3(_split3(p), x_bd)
        m *= 2
    return p


def _gdn_prompt_kernel(n_batch, n_heads, chunk, x_ref, gate_ref, sm_ref, cw_ref, alog_ref, dtb_ref, onorm_ref,
                       bd_ref, bd2_ref, o_ref, s_out_ref, xbuf_ref, s_ref):
    c_idx = pl.program_id(0)
    hw = n_heads * LANES
    cw = n_heads * chunk
    step_rows = x_ref.shape[1]
    n_chunks = step_rows // chunk

    @pl.when(c_idx == 0)
    def _():
        xbuf_ref[:, 0:SUBLANES, :] = jnp.zeros((n_batch, SUBLANES, xbuf_ref.shape[2]), F32)
        s_ref[...] = jnp.zeros(s_ref.shape, F32)

    r = lax.broadcasted_iota(jnp.int32, (chunk, chunk), 0)
    c = lax.broadcasted_iota(jnp.int32, (chunk, chunk), 1)
    incl_ones = jnp.where(c <= r, 1.0, 0.0)
    row_i = lax.broadcasted_iota(jnp.int32, (chunk, cw), 0)
    lane = lax.broadcasted_iota(jnp.int32, (chunk, cw), 1)
    lane_j = lane & (chunk - 1)
    lane_h = lax.shift_right_logical(lane, chunk.bit_length() - 1)
    incl_cat = lane_j <= row_i
    strict_cat = lane_j < row_i
    eye_cat = lane_j == row_i
    bd_mask = bd_ref[...]
    bd2_mask = bd2_ref[...]

    for b in range(n_batch):
        x = x_ref[b]
        xbuf_ref[b, SUBLANES:SUBLANES + step_rows, :] = x
        conv = cw_ref[CONV_K - 1:CONV_K, :] * x
        for i in range(CONV_K - 1):
            shift = CONV_K - 1 - i
            conv = conv + cw_ref[i:i + 1, :] * xbuf_ref[b, SUBLANES - shift:SUBLANES - shift + step_rows, :]
        xbuf_ref[b, 0:SUBLANES, :] = x[step_rows - SUBLANES:step_rows, :]
        act_all = _silu(conv)
        beta_all, g_all = _gdn_gates(sm_ref[b], alog_ref[...], dtb_ref[...])

        for cc in range(n_chunks):
            rows = slice(cc * chunk, (cc + 1) * chunk)
            act = act_all[rows]
            beta = beta_all[rows]
            g = g_all[rows]
            heads = range(n_heads)
            q = [_l2norm_rows(act[:, h * LANES:(h + 1) * LANES]) * (LANES ** -0.5) for h in heads]
            k = [_l2norm_rows(act[:, hw + h * LANES:hw + (h + 1) * LANES]) for h in heads]
            v = [act[:, 2 * hw + h * LANES:2 * hw + (h + 1) * LANES] for h in heads]
            b_col = [beta[:, h:h + 1] for h in heads]
            kb = [k[h] * b_col[h] for h in heads]

            g_cat = jnp.broadcast_to(g[:, 0:1], (chunk, cw))
            for h in range(1, n_heads):
                g_cat = jnp.where(lane_h == h, g[:, h:h + 1], g_cat)
            gcol_cat = _dot32(incl_ones, g_cat)
            grow_cat = jnp.sum(jnp.where(lane_j >= row_i, g_cat, 0.0), axis=0, keepdims=True)
            decay_cat = jnp.where(incl_cat, jnp.exp(jnp.where(incl_cat, gcol_cat - grow_cat, 0.0)), 0.0)
            gc_col = [gcol_cat[:, h * chunk:h * chunk + 1] for h in heads]
            egc = [jnp.exp(gc_col[h]) for h in heads]
            g_last = [gc_col[h][chunk - 1:chunk] for h in heads]

            k_stack = jnp.concatenate(k, axis=0)
            kk = _fold_diag_blocks(_dot3(_split3(jnp.concatenate(kb, axis=0)), _split3(k_stack), nt=True),
                                   n_heads, chunk)
            qk = _fold_diag_blocks(_dot_nt(jnp.concatenate(q, axis=0), k_stack), n_heads, chunk)
            t_cat = _inv_unit_lower_cat(jnp.where(strict_cat, kk * decay_cat, 0.0), eye_cat, n_heads, bd_mask)
            a_in = (qk * decay_cat).astype(BF16)

            rhs = jnp.concatenate([v[h] * b_col[h] for h in heads] + [kb[h] * egc[h] for h in heads], axis=1)
            uw = jnp.dot(t_cat.astype(BF16), jnp.concatenate([rhs.astype(BF16)] * n_heads, axis=0) * bd2_mask,
                         preferred_element_type=F32)

            s_old = [s_ref[b, h] for h in heads]
            v_new, o_inter = [], []
            for h in heads:
                wq = _dot(jnp.concatenate([uw[:, hw + h * LANES:hw + (h + 1) * LANES], q[h] * egc[h]], axis=0),
                          s_old[h])
                v_new.append(uw[:, h * LANES:(h + 1) * LANES] - wq[:chunk])
                o_inter.append(wq[chunk:])
            vn_cat = jnp.concatenate(v_new, axis=1).astype(BF16)
            o_intra = jnp.dot(a_in, jnp.concatenate([vn_cat] * n_heads, axis=0) * bd2_mask[:, :hw],
                              preferred_element_type=F32)
            for h in heads:
                cols = slice(h * LANES, (h + 1) * LANES)
                k_dec = k[h] * jnp.exp(g_last[h] - gc_col[h])
                v_pad = jnp.concatenate([v_new[h], jnp.zeros((LANES - chunk, LANES), F32)], axis=0)
                s_ref[b, h] = s_old[h] * jnp.exp(g_last[h]) + _dot(_transpose_rows(k_dec), v_pad)
                o_ref[b, rows, cols] = _gdn_out(o_inter[h] + o_intra[:, cols], onorm_ref[...],
                                                gate_ref[b, rows, cols])

    @pl.when(c_idx == pl.num_programs(0) - 1)
    def _():
        s_out_ref[...] = s_ref[...]


def _gdn_prompt(proj, n_batch, seq, n_heads, gate_blk, sm_blk, conv_w, alog, dtb, onorm):
    chunk = GDN_CHUNK
    step = min(GDN_CHUNKS_PER_STEP * chunk, seq)
    assert seq % step == 0 and step % chunk == 0
    qkv_w = 3 * n_heads * LANES
    hw = n_heads * LANES
    cw = n_heads * chunk
    proj3 = proj.reshape(n_batch, seq, proj.shape[1])
    blk_row = (jnp.arange(cw) // chunk)[:, None]
    bd_mask = (blk_row == (jnp.arange(cw) // chunk)[None, :]).astype(BF16)
    bd2_mask = (blk_row == ((jnp.arange(2 * hw) // LANES) % n_heads)[None, :]).astype(BF16)
    const = lambda a: pl.BlockSpec(a.shape, lambda c: (0,) * a.ndim)
    o, s_fin = pl.pallas_call(
        functools.partial(_gdn_prompt_kernel, n_batch, n_heads, chunk),
        grid=(seq // step,),
        in_specs=[pl.BlockSpec((n_batch, step, qkv_w), lambda c: (0, c, 0)),
                  pl.BlockSpec((n_batch, step, hw), lambda c: (0, c, gate_blk)),
                  pl.BlockSpec((n_batch, step, 2 * LANES), lambda c: (0, c, sm_blk)),
                  const(conv_w), const(alog), const(dtb), const(onorm), const(bd_mask), const(bd2_mask)],
        out_specs=[pl.BlockSpec((n_batch, step, hw), lambda c: (0, c, 0)),
                   pl.BlockSpec((n_batch, n_heads, LANES, LANES), lambda c: (0, 0, 0, 0))],
        out_shape=[jax.ShapeDtypeStruct((n_batch, seq, hw), F32),
                   jax.ShapeDtypeStruct((n_batch, n_heads, LANES, LANES), F32)],
        scratch_shapes=[pltpu.VMEM((n_batch, SUBLANES + step, qkv_w), F32),
                        pltpu.VMEM((n_batch, n_heads, LANES, LANES), F32)],
        compiler_params=_params(("arbitrary",)),
        name="gdn_prompt",
    )(proj3, proj3, proj3, conv_w, alog, dtb, onorm, bd_mask, bd2_mask)
    return o.reshape(n_batch * seq, hw), s_fin


def _gdn_sample_kernel(n_heads, n_tok, bb, xp_ref, gate_ref, sm_ref, cw_ref, alog_ref, dtb_ref, onorm_ref,
                       s0_ref, o_ref, s_out_ref):
    hw = n_heads * LANES

    def one_request(b, carry):
        conv = cw_ref[0:1, :] * xp_ref[b, 0:PAD_TOK, :]
        for i in range(1, CONV_K):
            conv = conv + cw_ref[i:i + 1, :] * xp_ref[b, i:i + PAD_TOK, :]
        act = _silu(conv)
        beta, g = _gdn_gates(sm_ref[b], alog_ref[...], dtb_ref[...])
        eg = jnp.exp(g)
        gate = gate_ref[b]
        ks = [_l2norm_rows(act[:, hw + h * LANES:hw + (h + 1) * LANES]) for h in range(n_heads)]
        qs = [_l2norm_rows(act[:, h * LANES:(h + 1) * LANES]) * (LANES ** -0.5) for h in range(n_heads)]
        kq_t = _transpose_rows(jnp.concatenate(ks + qs, axis=0))
        for h in range(n_heads):
            v_h = act[:, 2 * hw + h * LANES:2 * hw + (h + 1) * LANES]
            s = s0_ref[b, h]
            outs = []
            for t in range(n_tok):
                kc = kq_t[:, h * PAD_TOK + t:h * PAD_TOK + t + 1]
                qc = kq_t[:, (n_heads + h) * PAD_TOK + t:(n_heads + h) * PAD_TOK + t + 1]
                e = eg[t:t + 1, h:h + 1]
                ks = jnp.sum(s * kc, axis=0, keepdims=True)
                d = beta[t:t + 1, h:h + 1] * (v_h[t:t + 1, :] - e * ks)
                s = e * s + kc * d
                outs.append(jnp.sum(s * qc, axis=0, keepdims=True))
            outs.append(jnp.zeros((PAD_TOK - n_tok, LANES), F32))
            o = jnp.concatenate(outs, axis=0)
            s_out_ref[b, h] = s
            o_ref[b, :, h * LANES:(h + 1) * LANES] = _gdn_out(
                o, onorm_ref[...], gate[:, h * LANES:(h + 1) * LANES])
        return carry

    lax.fori_loop(0, bb, one_request, 0)


def _gdn_sample(xp, gate, sm, conv_w, alog, dtb, onorm, s0, n_tok, bb):
    nb, n_heads = s0.shape[0], s0.shape[1]
    hw = n_heads * LANES
    qkv_w = 3 * hw
    return pl.pallas_call(
        functools.partial(_gdn_sample_kernel, n_heads, n_tok, bb),
        grid=(nb // bb,),
        in_specs=[pl.BlockSpec((bb, 2 * PAD_TOK, qkv_w), lambda i: (i, 0, 0)),
                  pl.BlockSpec((bb, PAD_TOK, hw), lambda i: (i, 0, 0)),
                  pl.BlockSpec((bb, PAD_TOK, 2 * LANES), lambda i: (i, 0, 0)),
                  pl.BlockSpec((CONV_K, qkv_w), lambda i: (0, 0)),
                  pl.BlockSpec((1, LANES), lambda i: (0, 0)),
                  pl.BlockSpec((1, LANES), lambda i: (0, 0)),
                  pl.BlockSpec((1, LANES), lambda i: (0, 0)),
                  pl.BlockSpec((bb, n_heads, LANES, LANES), lambda i: (i, 0, 0, 0))],
        out_specs=[pl.BlockSpec((bb, PAD_TOK, hw), lambda i: (i, 0, 0)),
                   pl.BlockSpec((bb, n_heads, LANES, LANES), lambda i: (i, 0, 0, 0))],
        out_shape=[jax.ShapeDtypeStruct((nb, PAD_TOK, hw), F32),
                   jax.ShapeDtypeStruct(s0.shape, F32)],
        compiler_params=_params(("parallel",)),
        name="gdn_sample",
    )(xp, gate, sm, conv_w, alog, dtb, onorm, s0)


def _head_rms(r, gain, seg_ones, head_dim):
    ss = jnp.dot((r * r).astype(BF16), seg_ones, preferred_element_type=F32)
    return r * lax.rsqrt(ss * (1.0 / head_dim) + NORM_EPS) * gain


def _sb_qkv_kernel(head_dim, n_prev, transposed, x_ref, g_ref, w_ref, qn_ref, kn_ref, seg_ref, *refs):
    qb_ref, k_ref, v_ref, kb_ref, vb_ref, hn_ref = refs[n_prev:]
    j = pl.program_id(1)

    @pl.when(j == 0)
    def _():
        hn_ref[...] = _rms_rows(x_ref[...], g_ref[...]).astype(BF16)

    r = jnp.dot(hn_ref[...], w_ref[...], preferred_element_type=F32)
    seg = seg_ref[...]
    seg_w = seg.shape[0]
    n_blk = r.shape[1] // seg_w

    @pl.when(j == 0)
    def _():
        for c in range(n_blk):
            cols = slice(c * seg_w, (c + 1) * seg_w)
            qn = _head_rms(r[:, cols], qn_ref[...], seg, head_dim)
            qb_ref[:, cols] = (qn * (head_dim ** -0.5)).astype(BF16)

    @pl.when(j == 1)
    def _():
        for c in range(n_blk):
            cols = slice(c * seg_w, (c + 1) * seg_w)
            kn = _head_rms(r[:, cols], kn_ref[...], seg, head_dim)
            if transposed:
                k_ref[cols, :] = jnp.transpose(kn)
            else:
                k_ref[:, cols] = kn
            kb_ref[:, cols] = kn.astype(BF16)

    @pl.when(j == 2)
    def _():
        if transposed:
            for c in range(n_blk):
                cols = slice(c * seg_w, (c + 1) * seg_w)
                v_ref[cols, :] = jnp.transpose(r[:, cols])
        else:
            v_ref[...] = r
        vb_ref[...] = r.astype(BF16)


def _sb_qkv(x, gain, w, q_gain, k_gain, seg_ones, head_dim, tm, layer, n_layers, kv_stacks, seq=None):
    t, d = x.shape
    width = w.shape[1] // 3
    row_blk = pl.BlockSpec((tm, width), lambda i, j: (i, 0))
    if seq is None:
        stack_shape = (n_layers, t, width)
        stack_blk = pl.BlockSpec((None, tm, width), lambda i, j: (layer, i, 0))
    else:
        per_seq = seq // tm
        stack_shape = (n_layers, t // seq, width, seq)
        stack_blk = pl.BlockSpec((None, None, width, tm), lambda i, j: (layer, i // per_seq, 0, i % per_seq))
    n_fixed = 6
    return pl.pallas_call(
        functools.partial(_sb_qkv_kernel, head_dim, len(kv_stacks), seq is not None),
        grid=(t // tm, 3),
        in_specs=[pl.BlockSpec((tm, d), lambda i, j: (i, 0)),
                  pl.BlockSpec((1, d), lambda i, j: (0, 0)),
                  pl.BlockSpec((d, width), lambda i, j: (0, j)),
                  pl.BlockSpec(q_gain.shape, lambda i, j: (0, 0)),
                  pl.BlockSpec(k_gain.shape, lambda i, j: (0, 0)),
                  pl.BlockSpec(seg_ones.shape, lambda i, j: (0, 0))]
        + [pl.BlockSpec(memory_space=pl.ANY) for _ in kv_stacks],
        out_specs=[row_blk, stack_blk, stack_blk, row_blk, row_blk],
        out_shape=[jax.ShapeDtypeStruct((t, width), BF16),
                   jax.ShapeDtypeStruct(stack_shape, F32),
                   jax.ShapeDtypeStruct(stack_shape, F32),
                   jax.ShapeDtypeStruct((t, width), BF16),
                   jax.ShapeDtypeStruct((t, width), BF16)],
        input_output_aliases={n_fixed + s: 1 + s for s in range(len(kv_stacks))},
        scratch_shapes=[pltpu.VMEM((tm, d), BF16)],
        compiler_params=_params(("parallel", "arbitrary")),
        name="sb_qkv",
    )(x, gain, w, q_gain, k_gain, seg_ones, *kv_stacks)


def _attn_prompt_kernel(head_dim, bias_ref, q_ref, k_ref, v_ref, suf_ref, o_ref,
                        acc_ref, carry_ref, z_ref, cs_ref, rs_ref):
    g = pl.program_id(1)
    qi = pl.program_id(2)
    tq = q_ref.shape[0]
    tk = suf_ref.shape[0]
    n_pairs = q_ref.shape[1] // LANES
    rows = 2 * tq
    lane = lax.broadcasted_iota(jnp.int32, (tq, LANES), 1)
    first = lane < head_dim
    row1 = lax.broadcasted_iota(jnp.int32, (rows, 1), 0)
    qq, bias = [], []
    for pp in range(n_pairs):
        q = q_ref[:, pp * LANES:(pp + 1) * LANES].astype(F32)
        qq.append(jnp.concatenate([jnp.where(first, q, 0.0), jnp.where(first, 0.0, q)], axis=0).astype(BF16))
        head0 = 2 * (g * n_pairs + pp)
        bias.append(jnp.where(row1 < tq, bias_ref[head0], bias_ref[head0 + 1]))
    acc_ref[...] = jnp.zeros(acc_ref.shape, F32)
    carry_ref[...] = jnp.zeros(carry_ref.shape, F32)

    def causal_mask():
        r2 = lax.broadcasted_iota(jnp.int32, (rows, tk), 0)
        c2 = lax.broadcasted_iota(jnp.int32, (rows, tk), 1)
        return c2 < jnp.where(r2 >= tq, r2 - tq, r2)

    def logits(j, zs):
        start = pl.multiple_of(j * tk, tk)
        for pp in range(n_pairs):
            kblk = k_ref[pl.ds(start, tk), pp * LANES:(pp + 1) * LANES]
            z_ref[pp, zs] = lax.dot_general(qq[pp], kblk, (((1,), (1,)), ((), ())),
                                            preferred_element_type=F32) + bias[pp]

    def keeps(zs, cs_slot, masked):
        for pp in range(n_pairs):
            z = z_ref[pp, zs]
            sp = jnp.maximum(z, 0.0) + jnp.log(1.0 + jnp.exp2(jnp.abs(z) * (-LOG2E)))
            if masked:
                sp = jnp.where(causal_mask(), sp, 0.0)
            cs_ref[pp, cs_slot] = jnp.dot(sp.astype(BF16), suf_ref[...], preferred_element_type=F32)
            rs_ref[pp, cs_slot] = jnp.broadcast_to(-jnp.sum(sp, axis=1, keepdims=True), (rows, LANES))

    def weigh(j, zs, cs_slot, masked, live=None):
        start = pl.multiple_of(j * tk, tk)
        for pp in range(n_pairs):
            vblk = v_ref[pl.ds(start, tk), pp * LANES:(pp + 1) * LANES]
            carry = carry_ref[pp]
            a = jnp.exp(z_ref[pp, zs] + cs_ref[pp, cs_slot] + jnp.concatenate([carry] * (tk // LANES), axis=1))
            if masked:
                a = jnp.where(causal_mask(), a, 0.0)
            rs = rs_ref[pp, cs_slot]
            if live is not None:
                a = jnp.where(live, a, 0.0)
                rs = jnp.where(live, rs, 0.0)
            acc_ref[pp] += jnp.dot(a.astype(BF16), vblk, preferred_element_type=F32)
            carry_ref[pp] = carry + rs

    def next3(s):
        return jnp.where(s == 2, 0, s + 1)

    logits(qi, 2)
    logits(jnp.maximum(qi - 1, 0), 0)
    keeps(2, 1, True)
    logits(jnp.maximum(qi - 2, 0), 1)
    weigh(qi, 2, 1, True)
    keeps(0, 0, False)

    def trip(j, zc, cc):
        zb = next3(zc)
        weigh(j, zc, cc, False)
        keeps(zb, 1 - cc, False)
        logits(j - 2, next3(zb))
        return zb, 1 - cc

    def unrolled_trips(t, slots):
        for u in range(ATT_TRIPS_PER_BODY):
            slots = trip(qi - 1 - ATT_TRIPS_PER_BODY * t - u, *slots)
        return slots

    n_trips = jnp.maximum(qi - 2, 0)
    n_bodies = lax.shift_right_logical(n_trips, ATT_TRIPS_PER_BODY.bit_length() - 1)
    slots = lax.fori_loop(0, n_bodies, unrolled_trips, (0, 0))
    first_left = qi - 1 - ATT_TRIPS_PER_BODY * n_bodies
    zc, cc = lax.fori_loop(0, n_trips & (ATT_TRIPS_PER_BODY - 1),
                           lambda r, s: trip(first_left - r, *s), slots)

    zb = next3(zc)
    weigh(jnp.minimum(qi, 1), zc, cc, False, live=qi >= 2)
    keeps(zb, 1 - cc, False)
    weigh(0, zb, 1 - cc, False, live=qi >= 1)

    for pp in range(n_pairs):
        acc = acc_ref[pp]
        o_ref[:, pp * LANES:(pp + 1) * LANES] = jnp.where(first, acc[:tq], acc[tq:])


def _attn_prompt(qb, kb, vb, bias, suffix_ones, n_batch, seq, head_dim):
    width = qb.shape[1]
    tq = suffix_ones.shape[0]
    nq = seq // tq
    npp = ATT_PAIRS_PER_STEP
    blk_w = npp * LANES
    assert LANES == 2 * head_dim and seq % tq == 0 and width % blk_w == 0
    return pl.pallas_call(
        functools.partial(_attn_prompt_kernel, head_dim),
        grid=(n_batch, width // blk_w, nq),
        in_specs=[pl.BlockSpec(memory_space=pltpu.SMEM),
                  pl.BlockSpec((tq, blk_w), lambda b, p, i: (b * nq + i, p)),
                  pl.BlockSpec((seq, blk_w), lambda b, p, i: (b, p)),
                  pl.BlockSpec((seq, blk_w), lambda b, p, i: (b, p)),
                  pl.BlockSpec(suffix_ones.shape, lambda b, p, i: (0, 0))],
        out_specs=pl.BlockSpec((tq, blk_w), lambda b, p, i: (b * nq + i, p)),
        out_shape=jax.ShapeDtypeStruct((n_batch * seq, width), F32),
        scratch_shapes=[pltpu.VMEM((npp, 2 * tq, LANES), F32), pltpu.VMEM((npp, 2 * tq, LANES), F32),
                        pltpu.VMEM((npp, 3, 2 * tq, tq), F32), pltpu.VMEM((npp, 2, 2 * tq, tq), F32),
                        pltpu.VMEM((npp, 2, 2 * tq, LANES), F32)],
        compiler_params=_params(("parallel", "parallel", "arbitrary")),
        name="attn_prompt",
    )(bias, qb, kb, vb, suffix_ones)


def _attn_sample_kernel(head_dim, n_tok, n_kv, pt_ref, qbd_ref, kn_ref, vn_ref, *refs):
    kv_refs = refs[:n_kv]
    bias_ref, suf_ref, o_ref, acc_ref, carry_ref = refs[n_kv:]
    j = pl.program_id(1)
    n_rows, width = acc_ref.shape
    row_head = lax.shift_right_logical(lax.broadcasted_iota(jnp.int32, (n_rows, width), 0),
                                       n_tok.bit_length() - 1)
    col_head = lax.shift_right_logical(lax.broadcasted_iota(jnp.int32, (n_rows, width), 1),
                                       head_dim.bit_length() - 1)
    own = row_head == col_head
    bias = bias_ref[...]
    qbd = qbd_ref[0]

    @pl.when(j == 0)
    def _():
        qf = qbd.astype(F32)
        tok = lax.broadcasted_iota(jnp.int32, (n_rows, 1), 0) & (n_tok - 1)
        bcol = bias[:, 0:1]
        lks, lss, valids = [], [], []
        for s in range(n_tok):
            z = jnp.sum(qf * kn_ref[0, s:s + 1, :], axis=-1, keepdims=True) + bcol
            sp = _softplus(z)
            valid = tok > s
            valids.append(valid)
            lks.append(jnp.where(valid, -sp, 0.0))
            lss.append(z - sp)
        later = jnp.zeros((n_rows, 1), F32)
        acc = jnp.zeros((n_rows, width), F32)
        for s in reversed(range(n_tok)):
            a = jnp.where(valids[s], jnp.exp(lss[s] + later), 0.0)
            acc = acc + a * vn_ref[0, s:s + 1, :]
            later = later + lks[s]
        acc_ref[...] = acc
        carry_ref[...] = jnp.broadcast_to(later, carry_ref.shape)

    k_refs, v_refs = kv_refs[0::2], kv_refs[1::2]
    zs = [jnp.dot(qbd, k_ref[...].astype(BF16), preferred_element_type=F32) + bias for k_ref in k_refs]
    sps = [jnp.maximum(z, 0.0) + jnp.log(1.0 + jnp.exp2(jnp.abs(z) * (-LOG2E))) for z in zs]
    css = [jnp.dot(sp.astype(BF16), suf_ref[...], preferred_element_type=F32) for sp in sps]
    carry = carry_ref[...]
    acc = acc_ref[...]
    for z, sp, cs, v_ref in zip(zs, sps, css, v_refs):
        a = jnp.exp(z + cs + carry)
        acc = acc + lax.dot_general(a.astype(BF16), v_ref[...].astype(BF16), (((1,), (1,)), ((), ())),
                                    preferred_element_type=F32)
        carry = carry - jnp.sum(sp, axis=1, keepdims=True)
    carry_ref[...] = carry
    acc_ref[...] = acc

    @pl.when(j == pl.num_programs(1) - 1)
    def _():
        own_acc = jnp.where(own, acc_ref[...], 0.0)
        sel_r = lax.broadcasted_iota(jnp.int32, (SUBLANES, n_rows), 0)
        sel_c = lax.broadcasted_iota(jnp.int32, (SUBLANES, n_rows), 1) & (n_tok - 1)
        o = _dot32(jnp.where(sel_r == sel_c, 1.0, 0.0), own_acc)
        o_ref[0] = o[:n_tok]


def _attn_sample(page_table, qbd, k_new, v_new, cache_k, cache_v, layer, bias_rows, suffix_ones,
                 head_dim, n_tok):
    nb, n_pages = page_table.shape
    n_layers, n_pool, page, n_heads, _ = cache_k.shape
    width = n_heads * head_dim
    ck = jnp.transpose(cache_k, (0, 1, 3, 4, 2)).reshape(n_layers * n_pool, width, page)
    cv = jnp.transpose(cache_v, (0, 1, 3, 4, 2)).reshape(n_layers * n_pool, width, page)
    n_rows = n_heads * n_tok
    base = layer * n_pool
    per_step = min(SAMPLE_PAGES_PER_STEP, n_pages)
    assert n_pages % per_step == 0 and n_tok & (n_tok - 1) == 0 and n_tok <= SUBLANES

    def page_spec(g):
        def page_idx(b, j, pt):
            return (base + pt[b * n_pages + (n_pages - 1 - (j * per_step + g))], 0, 0)
        return pl.BlockSpec((None, width, page), page_idx)

    kv_specs, kv_args = [], []
    for g in range(per_step):
        kv_specs += [page_spec(g), page_spec(g)]
        kv_args += [ck, cv]
    tok_blk = pl.BlockSpec((1, n_tok, width), lambda b, j, pt: (b, 0, 0))
    grid_spec = pltpu.PrefetchScalarGridSpec(
        num_scalar_prefetch=1,
        grid=(nb, n_pages // per_step),
        in_specs=[pl.BlockSpec((1, n_rows, width), lambda b, j, pt: (b, 0, 0)), tok_blk, tok_blk] + kv_specs + [
            pl.BlockSpec((n_rows, LANES), lambda b, j, pt: (0, 0)),
            pl.BlockSpec(suffix_ones.shape, lambda b, j, pt: (0, 0))],
        out_specs=tok_blk,
        scratch_shapes=[pltpu.VMEM((n_rows, width), F32),
                        pltpu.VMEM((n_rows, LANES), F32)],
    )
    return pl.pallas_call(
        functools.partial(_attn_sample_kernel, head_dim, n_tok, 2 * per_step),
        grid_spec=grid_spec,
        out_shape=jax.ShapeDtypeStruct((nb, n_tok, width), F32),
        compiler_params=_params(("parallel", "arbitrary")),
        name="attn_sample",
    )(page_table.reshape(-1), qbd, k_new, v_new, *kv_args, bias_rows, suffix_ones)


def _pad_lanes(x, width):
    return jnp.pad(x, [(0, 0)] * (x.ndim - 1) + [(0, width - x.shape[-1])])


def _pad_tokens(x, total):
    return jnp.pad(x, [(0, 0), (0, total - x.shape[1]), (0, 0)])


def kernel(x_prompt, x_sample, cache_k, cache_v, state_delta, state_conv, page_table, norm_mix, norm_mlp,
           ab_w_in, ab_w_out, gmlp_w_s, gmlp_b_s, gmlp_v_norm, gdn_conv_w, gdn_a_log, gdn_dt_bias,
           gdn_out_norm, sb_w_qkv, sb_w_out, sb_q_norm, sb_k_norm, sb_logit_bias, mlp_w_up, mlp_w_down):
    n_batch, seq, d_model = x_prompt.shape
    nb_s, n_tok, _ = x_sample.shape
    depth = norm_mix.shape[0]
    n_groups = gmlp_w_s.shape[1]
    a_width = n_groups * LANES
    n_bheads = state_delta.shape[2]
    b_qkv = state_conv.shape[-1]
    b_width = n_bheads * LANES
    c_heads, head_dim = cache_k.shape[3], cache_k.shape[4]
    n_prompt = n_batch * seq
    n_sample = nb_s * n_tok
    assert a_width == b_width and b_qkv == 3 * b_width and LANES % head_dim == 0
    assert seq % ROW_TILE == 0 and n_sample % ROW_TILE == 0 and n_sample % GMLP_CHUNK == 0
    assert GMLP_CHUNK % n_tok == 0 and n_tok <= PAD_TOK

    y_p = x_prompt.reshape(n_prompt, d_model)
    y_s = x_sample.reshape(n_sample, d_model)
    assert cache_k.shape[2] == PAGE_BLOCK
    pos = jnp.arange(PAGE_BLOCK)
    page_suffix_ones = -(pos[:, None] >= pos[None, :]).astype(BF16)
    blk = jnp.arange(min(ATT_BLOCK, seq))
    suffix_ones = -(blk[:, None] >= blk[None, :]).astype(BF16)
    seg = jnp.arange(MXU_TILE) // head_dim
    seg_ones = (seg[:, None] == seg[None, :]).astype(BF16)

    n_c_layers = depth // 2
    kv_pr, kv_sa = (), ()
    d_pr, d_sa, c_pr, c_sa, gv_sa = [], [], [], [], []
    for layer in range(depth):
        li = layer // 2
        gain_mix = norm_mix[layer][None, :]
        if layer % 2 == 0:
            w_in = ab_w_in[li]
            o = 0
            a_u, o = w_in[:, o:o + a_width], o + a_width
            a_v, o = w_in[:, o:o + a_width], o + a_width
            qkv_w, o = w_in[:, o:o + b_qkv], o + b_qkv
            beta_w, o = w_in[:, o:o + n_bheads], o + n_bheads
            araw_w, o = w_in[:, o:o + n_bheads], o + n_bheads
            gate_w = w_in[:, o:]
            w_cat = jnp.concatenate([qkv_w, a_u, a_v, gate_w, _pad_lanes(beta_w, LANES),
                                     _pad_lanes(araw_w, LANES)], axis=1).astype(BF16)
            u_blk = b_qkv // a_width
            v_blk = u_blk + 1
            gate_blk = u_blk + 2
            sm_blk = (b_qkv + 3 * a_width) // (2 * LANES)
            tn = w_cat.shape[1] // 2
            w_out = ab_w_out[li].astype(BF16)
            alog = _pad_lanes(gdn_a_log[li][None, :], LANES)
            dtb = _pad_lanes(gdn_dt_bias[li][None, :], LANES)
            onorm = gdn_out_norm[li][None, :]
            conv_w = gdn_conv_w[li]
            v_norm = gmlp_v_norm[li][None, :]
            tril = jnp.tril(jnp.ones((GMLP_CHUNK, GMLP_CHUNK), bool))
            wmix_p = jnp.where(tril, gmlp_w_s[li], 0.0).astype(BF16)
            bias_p = jnp.broadcast_to(gmlp_b_s[li][:, :, None], (n_groups, GMLP_CHUNK, LANES))
            small = jnp.where(tril[:n_tok, :n_tok], gmlp_w_s[li][:, :n_tok, :n_tok], 0.0)
            eye_req = jnp.eye(GMLP_CHUNK // n_tok, dtype=F32)
            wmix_s = jnp.einsum('ab,gij->gaibj', eye_req, small).reshape(
                n_groups, GMLP_CHUNK, GMLP_CHUNK).astype(BF16)
            bias_s = jnp.broadcast_to(
                jnp.tile(gmlp_b_s[li][:, :n_tok], (1, GMLP_CHUNK // n_tok))[:, :, None],
                (n_groups, GMLP_CHUNK, LANES))

            proj_p = _norm_matmul(y_p, gain_mix, w_cat, ROW_TILE, tn)
            oa_p, _ = _gmlp(proj_p, u_blk, v_blk, v_norm, wmix_p, bias_p, ROW_TILE)
            ob_p, s_p = _gdn_prompt(proj_p, n_batch, seq, n_bheads, gate_blk, sm_blk, conv_w, alog, dtb, onorm)
            c_pr.append(proj_p[:, :b_qkv].reshape(n_batch, seq, b_qkv)[:, seq - (CONV_K - 1):])
            d_pr.append(s_p)
            y_p = _matmul_residual(y_p, [oa_p, ob_p], [w_out[:a_width], w_out[a_width:]], ROW_TILE)

            proj_s = _norm_matmul(y_s, gain_mix, w_cat, ROW_TILE, tn)
            oa_s, vn_s = _gmlp(proj_s, u_blk, v_blk, v_norm, wmix_s, bias_s, ROW_TILE)
            proj_s3 = proj_s.reshape(nb_s, n_tok, -1)
            qkv_s = proj_s3[:, :, :b_qkv]
            xp = _pad_tokens(jnp.concatenate([state_conv[li], qkv_s], axis=1), 2 * PAD_TOK)
            gate_s = _pad_tokens(proj_s3[:, :, gate_blk * a_width:(gate_blk + 1) * a_width], PAD_TOK)
            sm_s = _pad_tokens(proj_s3[:, :, sm_blk * 2 * LANES:], PAD_TOK)
            ob_s, s_s = _gdn_sample(xp, gate_s, sm_s, conv_w, alog, dtb, onorm, state_delta[li], n_tok, 8)
            ob_s = ob_s[:, :n_tok].reshape(n_sample, b_width)
            c_sa.append(qkv_s[:, n_tok - (CONV_K - 1):])
            d_sa.append(s_s)
            gv_sa.append(vn_s.reshape(nb_s, n_tok, a_width))
            y_s = _matmul_residual(y_s, [oa_s, ob_s], [w_out[:a_width], w_out[a_width:]], ROW_TILE)
        else:
            w_qkv = sb_w_qkv[li].astype(BF16)
            w_out = sb_w_out[li].astype(BF16)
            width = c_heads * head_dim
            q_gain = jnp.tile(sb_q_norm[li], MXU_TILE // head_dim)[None, :]
            k_gain = jnp.tile(sb_k_norm[li], MXU_TILE // head_dim)[None, :]
            bias = sb_logit_bias[li]

            qb, k_pr, v_pr, kb, vb = _sb_qkv(y_p, gain_mix, w_qkv, q_gain, k_gain, seg_ones, head_dim, ROW_TILE,
                                             li, n_c_layers, kv_pr, seq=seq)
            kv_pr = (k_pr, v_pr)
            o_p = _attn_prompt(qb, kb, vb, bias, suffix_ones, n_batch, seq, head_dim)
            y_p = _matmul_residual(y_p, [o_p], [w_out], ROW_TILE)

            qb, k_sa, v_sa, _, _ = _sb_qkv(y_s, gain_mix, w_qkv, q_gain, k_gain, seg_ones, head_dim, ROW_TILE,
                                           li, n_c_layers, kv_sa)
            kv_sa = (k_sa, v_sa)
            k_f, v_f = k_sa[li], v_sa[li]
            head_cols = (jnp.arange(width) // head_dim)[None, :] == jnp.arange(c_heads)[:, None]
            q_s = qb.reshape(nb_s, 1, n_tok, width)
            qbd = jnp.where(head_cols[None, :, None, :], q_s, jnp.zeros_like(q_s)).reshape(
                nb_s, c_heads * n_tok, width)
            bias_rows = jnp.broadcast_to(jnp.repeat(bias, n_tok)[:, None], (c_heads * n_tok, LANES))
            o_s = _attn_sample(page_table, qbd, k_f.reshape(nb_s, n_tok, width), v_f.reshape(nb_s, n_tok, width),
                               cache_k, cache_v, li, bias_rows, page_suffix_ones, head_dim, n_tok)
            o_s = o_s.reshape(n_sample, width)
            y_s = _matmul_residual(y_s, [o_s], [w_out], ROW_TILE)

        gain_mlp = norm_mlp[layer][None, :]
        w_up = mlp_w_up[layer].astype(BF16)
        w_down = mlp_w_down[layer].astype(BF16)
        y_p = _mlp(y_p, gain_mlp, w_up, w_down, MLP_ROW_TILE if n_prompt % MLP_ROW_TILE == 0 else ROW_TILE, 1024)
        y_s = _mlp(y_s, gain_mlp, w_up, w_down, ROW_TILE, 1024)

    return (y_p.reshape(n_batch, seq, d_model), y_s.reshape(nb_s, n_tok, d_model),
            k_pr.reshape(n_c_layers, n_batch, c_heads, head_dim, seq).transpose(0, 1, 4, 2, 3),
            v_pr.reshape(n_c_layers, n_batch, c_heads, head_dim, seq).transpose(0, 1, 4, 2, 3),
            k_sa.reshape(n_c_layers, nb_s, n_tok, c_heads, head_dim),
            v_sa.reshape(n_c_layers, nb_s, n_tok, c_heads, head_dim),
            jnp.stack(d_pr), jnp.stack(d_sa), jnp.stack(c_pr), jnp.stack(c_sa), jnp.stack(gv_sa))
```
